```python
import jax, jax.numpy as jnp
from jax import lax
import numpy as np

D_MODEL = 4096
BATCH = 2
SEQ = 4096
DEPTH = 2
DEC_BATCH = 4
DEC_SEQ = 2048
PAST_LEN = 128

CONV_WIDTH = D_MODEL // 2
CONV_K = 3
SG_WIDTH = D_MODEL // 2
SG_GROUPS = 8
SG_GROUP_DIM = SG_WIDTH // SG_GROUPS
CHUNK = 128
IN_COLS = 3 * CONV_WIDTH + 2 * SG_WIDTH + 2 * D_MODEL
N_KEYS = 128
N_EXPERTS = N_KEYS * N_KEYS
PEER_HEADS = 8
PEER_TOPK = 16
D_KEY = 256
HALF_KEY = D_KEY // 2
TOKEN_BLOCK = 128
EPS = 1e-6

kernel_name = "hybrid_conv_sgu_peer_encoder"


def rms_norm(x, g):
    xf = x.astype(jnp.float32)
    y = xf * lax.rsqrt(jnp.mean(xf * xf, axis=-1, keepdims=True) + EPS)
    return (y * g.astype(jnp.float32)).astype(x.dtype)


def layer_norm(x, g, b):
    xf = x.astype(jnp.float32)
    mu = jnp.mean(xf, axis=-1, keepdims=True)
    xc = xf - mu
    y = xc * lax.rsqrt(jnp.mean(xc * xc, axis=-1, keepdims=True) + EPS)
    return (y * g.astype(jnp.float32) + b.astype(jnp.float32)).astype(x.dtype)


def short_conv_mixer(x_in, gate_b, gate_c, conv_w):
    z = gate_c * x_in
    s = z.shape[1]
    zp = jnp.pad(z, ((0, 0), (1, 1), (0, 0)))
    conv = zp[:, 0:s] * conv_w[0] + zp[:, 1:s + 1] * conv_w[1] + zp[:, 2:s + 2] * conv_w[2]
    return gate_b * conv


def spatial_gating(u, v, ln_g, ln_b, w_s, b_s):
    bt, s, _ = u.shape
    v = layer_norm(v, ln_g, ln_b)
    vc = v.reshape(bt, s // CHUNK, CHUNK, SG_GROUPS, SG_GROUP_DIM)
    mixed = jnp.einsum('gqp,bnpgc->bnqgc', w_s, vc) + b_s.T[None, None, :, :, None]
    return u * mixed.reshape(bt, s, SG_WIDTH)


def mixer_block(h, w_in, conv_w, sg_ln_g, sg_ln_b, sg_w, sg_b, proj_a, proj_b, w_o):
    z = h @ w_in
    c0 = CONV_WIDTH
    c1 = 2 * CONV_WIDTH
    c2 = 3 * CONV_WIDTH
    c3 = c2 + 2 * SG_WIDTH
    c4 = c3 + D_MODEL
    x_in, gate_b, gate_c, uv, g_a, g_b = jnp.split(z, [c0, c1, c2, c3, c4], axis=-1)
    y_a = short_conv_mixer(x_in, gate_b, gate_c, conv_w)
    uv = jax.nn.gelu(uv, approximate=False)
    u, v = jnp.split(uv, 2, axis=-1)
    y_b = spatial_gating(u, v, sg_ln_g, sg_ln_b, sg_w, sg_b)
    merged = jax.nn.sigmoid(g_a) * (y_a @ proj_a) + jax.nn.sigmoid(g_b) * (y_b @ proj_b)
    return merged @ w_o


def peer(x, w_q, sub_keys, u_tab, v_tab):
    bt, s, d = x.shape
    t = bt * s
    xt = x.reshape(t, d)
    q = (xt @ w_q).reshape(t, PEER_HEADS, 2, HALF_KEY)
    scores = jnp.einsum('thsk,hsnk->thsn', q, sub_keys).astype(jnp.float32)
    s1, i1 = lax.top_k(scores[:, :, 0], PEER_TOPK)
    s2, i2 = lax.top_k(scores[:, :, 1], PEER_TOPK)
    cand = (s1[..., :, None] + s2[..., None, :]).reshape(t, PEER_HEADS, PEER_TOPK * PEER_TOPK)
    top, pos = lax.top_k(cand, PEER_TOPK)
    e_idx = (jnp.take_along_axis(i1, pos // PEER_TOPK, axis=-1) * N_KEYS
             + jnp.take_along_axis(i2, pos % PEER_TOPK, axis=-1))
    gates = jax.nn.softmax(top, axis=-1)

    def expert_block(args):
        xb, eb, gb = args
        ub = u_tab[eb]
        pre = jnp.einsum('thkd,td->thk', ub, xb)
        act = (gb * jax.nn.gelu(pre.astype(jnp.float32), approximate=False)).astype(xb.dtype)
        vb = v_tab[eb]
        return jnp.einsum('thk,thkd->td', act, vb)

    nb = t // TOKEN_BLOCK
    out = lax.map(expert_block, (xt.reshape(nb, TOKEN_BLOCK, d),
                                 e_idx.reshape(nb, TOKEN_BLOCK, PEER_HEADS, PEER_TOPK),
                                 gates.reshape(nb, TOKEN_BLOCK, PEER_HEADS, PEER_TOPK)))
    return out.reshape(bt, s, d)


def trunk(x, norm1_g, w_in, conv_w, sg_ln_g, sg_ln_b, sg_w, sg_b, proj_a, proj_b, w_o,
          norm2_g, peer_wq, peer_keys, peer_u, peer_v, final_g):
    for l in range(DEPTH):
        h = rms_norm(x, norm1_g[l])
        x = x + mixer_block(h, w_in[l], conv_w[l], sg_ln_g[l], sg_ln_b[l], sg_w[l], sg_b[l],
                            proj_a[l], proj_b[l], w_o[l])
        h = rms_norm(x, norm2_g[l])
        x = x + peer(h, peer_wq[l], peer_keys[l], peer_u[l], peer_v[l])
    return rms_norm(x, final_g)


def setup_inputs(seed: int = 0) -> dict:
    key = jax.random.key(seed)
    ks = jax.random.split(key, 20)
    f = jnp.float32
    nrm = lambda k, shape, scale: jax.random.normal(k, shape, f) * scale
    return {
        "x_prompt": nrm(ks[0], (BATCH, SEQ, D_MODEL), 1.0),
        "x_sample": nrm(ks[1], (DEC_BATCH, DEC_SEQ, D_MODEL), 1.0),
        "norm1_g": 1.0 + nrm(ks[2], (DEPTH, D_MODEL), 0.02),
        "w_in": nrm(ks[3], (DEPTH, D_MODEL, IN_COLS), D_MODEL ** -0.5),
        "conv_w": nrm(ks[4], (DEPTH, CONV_K, CONV_WIDTH), CONV_K ** -0.5),
        "sg_ln_g": 1.0 + nrm(ks[5], (DEPTH, SG_WIDTH), 0.02),
        "sg_ln_b": nrm(ks[6], (DEPTH, SG_WIDTH), 0.02),
        "sg_w": nrm(ks[7], (DEPTH, SG_GROUPS, CHUNK, CHUNK), CHUNK ** -0.5),
        "sg_b": 1.0 + nrm(ks[8], (DEPTH, SG_GROUPS, CHUNK), 0.02),
        "proj_a": nrm(ks[9], (DEPTH, CONV_WIDTH, D_MODEL), CONV_WIDTH ** -0.5),
        "proj_b": nrm(ks[10], (DEPTH, SG_WIDTH, D_MODEL), SG_WIDTH ** -0.5),
        "w_o": nrm(ks[11], (DEPTH, D_MODEL, D_MODEL), D_MODEL ** -0.5),
        "norm2_g": 1.0 + nrm(ks[12], (DEPTH, D_MODEL), 0.02),
        "peer_wq": nrm(ks[13], (DEPTH, D_MODEL, PEER_HEADS * D_KEY), D_MODEL ** -0.5),
        "peer_keys": nrm(ks[14], (DEPTH, PEER_HEADS, 2, N_KEYS, HALF_KEY), HALF_KEY ** -0.5),
        "peer_u": nrm(ks[15], (DEPTH, N_EXPERTS, D_MODEL), D_MODEL ** -0.5),
        "peer_v": nrm(ks[16], (DEPTH, N_EXPERTS, D_MODEL), PEER_HEADS ** -0.5),
        "final_g": 1.0 + nrm(ks[17], (D_MODEL,), 0.02),
    }


def reference(x_prompt, x_sample, norm1_g, w_in, conv_w, sg_ln_g, sg_ln_b, sg_w, sg_b,
              proj_a, proj_b, w_o, norm2_g, peer_wq, peer_keys, peer_u, peer_v, final_g):
    y_prompt = trunk(x_prompt, norm1_g, w_in, conv_w, sg_ln_g, sg_ln_b, sg_w, sg_b, proj_a,
                     proj_b, w_o, norm2_g, peer_wq, peer_keys, peer_u, peer_v, final_g)
    y_sample = trunk(x_sample, norm1_g, w_in, conv_w, sg_ln_g, sg_ln_b, sg_w, sg_b, proj_a,
                     proj_b, w_o, norm2_g, peer_wq, peer_keys, peer_u, peer_v, final_g)
    return (y_prompt, y_sample)
```

```python
import functools
import math

import jax
import jax.numpy as jnp
from jax import lax
from jax.experimental import pallas as pl
from jax.experimental.pallas import tpu as pltpu

PEER_TOPK = 16
EPS = 1e-6
V7X_VMEM_LIMIT_BYTES = 56 * 1024 * 1024
BF16_SUBLANES = 16
LANES = 128
W_PITCH = 136

_F32 = jnp.float32
_BF16 = jnp.bfloat16


def _params(*semantics):
    return pltpu.CompilerParams(dimension_semantics=semantics,
                                vmem_limit_bytes=V7X_VMEM_LIMIT_BYTES)


def _blk(n, target):
    if n <= target:
        return n
    for b in range(target - target % LANES, 0, -LANES):
        if n % b == 0:
            return b
    raise ValueError(f"no lane-aligned block for {n} <= {target}")


def _gelu(x):
    return 0.5 * x * (1.0 + lax.erf(x * (1.0 / math.sqrt(2.0))))


def _rmsnorm_kernel(x_ref, g_ref, h_ref):
    x = x_ref[...]
    ms = jnp.mean(x * x, axis=-1, keepdims=True)
    h_ref[...] = (x * lax.rsqrt(ms + EPS) * g_ref[...]).astype(h_ref.dtype)


def _rmsnorm(x, g, out_dtype):
    t, d = x.shape
    bm = _blk(t, 512)
    return pl.pallas_call(
        _rmsnorm_kernel,
        grid=(t // bm,),
        in_specs=[pl.BlockSpec((bm, d), lambda i: (i, 0)),
                  pl.BlockSpec((1, d), lambda i: (0, 0))],
        out_specs=pl.BlockSpec((bm, d), lambda i: (i, 0)),
        out_shape=jax.ShapeDtypeStruct((t, d), out_dtype),
        compiler_params=_params("parallel"),
        name="rmsnorm",
    )(x, g.reshape(1, d))


def _add_rmsnorm_kernel(x_ref, d_ref, g_ref, xo_ref, h_ref):
    x = x_ref[...] + d_ref[...]
    xo_ref[...] = x
    ms = jnp.mean(x * x, axis=-1, keepdims=True)
    h_ref[...] = (x * lax.rsqrt(ms + EPS) * g_ref[...]).astype(h_ref.dtype)


def _add_rmsnorm(x, delta, g, out_dtype):
    t, d = x.shape
    bm = _blk(t, 256)
    return pl.pallas_call(
        _add_rmsnorm_kernel,
        grid=(t // bm,),
        in_specs=[pl.BlockSpec((bm, d), lambda i: (i, 0)),
                  pl.BlockSpec((bm, d), lambda i: (i, 0)),
                  pl.BlockSpec((1, d), lambda i: (0, 0))],
        out_specs=[pl.BlockSpec((bm, d), lambda i: (i, 0)),
                   pl.BlockSpec((bm, d), lambda i: (i, 0))],
        out_shape=[jax.ShapeDtypeStruct((t, d), _F32),
                   jax.ShapeDtypeStruct((t, d), out_dtype)],
        compiler_params=_params("parallel"),
        name="add_rmsnorm",
    )(x, delta, g.reshape(1, d))


def _mm_kernel(x_ref, w_ref, o_ref):
    o_ref[...] = jnp.dot(x_ref[...], w_ref[...],
                         preferred_element_type=_F32).astype(o_ref.dtype)


def _mm(x, w, out_dtype, bm=1024, bn=1024):
    m, k = x.shape
    n = w.shape[1]
    bm, bn = _blk(m, bm), _blk(n, bn)
    return pl.pallas_call(
        _mm_kernel,
        grid=(m // bm, n // bn),
        in_specs=[pl.BlockSpec((bm, k), lambda i, j: (i, 0)),
                  pl.BlockSpec((k, bn), lambda i, j: (0, j))],
        out_specs=pl.BlockSpec((bm, bn), lambda i, j: (i, j)),
        out_shape=jax.ShapeDtypeStruct((m, n), out_dtype),
        compiler_params=_params("parallel", "parallel"),
        name="matmul",
    )(x, w)


def _mm_res_kernel(x_ref, w_ref, r_ref, o_ref):
    o_ref[...] = r_ref[...] + jnp.dot(x_ref[...], w_ref[...], preferred_element_type=_F32)


def _mm_res(x, w, res, bm=1024, bn=1024):
    m, k = x.shape
    n = w.shape[1]
    bm, bn = _blk(m, bm), _blk(n, bn)
    return pl.pallas_call(
        _mm_res_kernel,
        grid=(m // bm, n // bn),
        in_specs=[pl.BlockSpec((bm, k), lambda i, j: (i, 0)),
                  pl.BlockSpec((k, bn), lambda i, j: (0, j)),
                  pl.BlockSpec((bm, bn), lambda i, j: (i, j))],
        out_specs=pl.BlockSpec((bm, bn), lambda i, j: (i, j)),
        out_shape=jax.ShapeDtypeStruct((m, n), _F32),
        compiler_params=_params("parallel", "parallel"),
        name="matmul_residual",
    )(x, w, res)


def _merge_kernel(ya_ref, yb_ref, pa_ref, pb_ref, ga_ref, gb_ref, o_ref):
    a = jnp.dot(ya_ref[...], pa_ref[...], preferred_element_type=_F32)
    b = jnp.dot(yb_ref[...], pb_ref[...], preferred_element_type=_F32)
    ga = jax.nn.sigmoid(ga_ref[...].astype(_F32))
    gb = jax.nn.sigmoid(gb_ref[...].astype(_F32))
    o_ref[...] = (ga * a + gb * b).astype(o_ref.dtype)


def _merge(ya, yb, pa, pb, z, gate_col0, bm=1024, bn=1024):
    m, k = ya.shape
    n = pa.shape[1]
    bm, bn = _blk(m, bm), _blk(math.gcd(n, gate_col0), bn)
    ga0 = gate_col0 // bn
    gb0 = (gate_col0 + n) // bn
    return pl.pallas_call(
        _merge_kernel,
        grid=(m // bm, n // bn),
        in_specs=[pl.BlockSpec((bm, k), lambda i, j: (i, 0)),
                  pl.BlockSpec((bm, k), lambda i, j: (i, 0)),
                  pl.BlockSpec((k, bn), lambda i, j: (0, j)),
                  pl.BlockSpec((k, bn), lambda i, j: (0, j)),
                  pl.BlockSpec((bm, bn), lambda i, j: (i, ga0 + j)),
                  pl.BlockSpec((bm, bn), lambda i, j: (i, gb0 + j))],
        out_specs=pl.BlockSpec((bm, bn), lambda i, j: (i, j)),
        out_shape=jax.ShapeDtypeStruct((m, n), _BF16),
        compiler_params=_params("parallel", "parallel"),
        name="merge_matmul",
    )(ya, yb, pa, pb, z, z)


def _mixer_kernel(xin_ref, gb_ref, gc_ref, u_ref, v_ref,
                  xin_p_ref, gc_p_ref, xin_n_ref, gc_n_ref,
                  cw_ref, lng_ref, lnb_ref, sw_ref, sb_ref,
                  ya_ref, yb_ref, *, seq_len, chunk, groups):
    i = pl.program_id(0)
    tm, width = xin_ref.shape
    zc = gc_ref[...].astype(_F32) * xin_ref[...].astype(_F32)
    zp = gc_p_ref[...].astype(_F32) * xin_p_ref[...].astype(_F32)
    zn = gc_n_ref[...].astype(_F32) * xin_n_ref[...].astype(_F32)
    row0 = i * tm
    has_prev = (row0 % seq_len != 0).astype(_F32)
    has_next = ((row0 + tm) % seq_len != 0).astype(_F32)
    halo_p = zp[BF16_SUBLANES - 1:BF16_SUBLANES, :] * has_prev
    halo_n = zn[0:1, :] * has_next
    rows = lax.broadcasted_iota(jnp.int32, (tm, width), 0)
    prev = jnp.where(rows == 0, halo_p, pltpu.roll(zc, 1, 0))
    nxt = jnp.where(rows == tm - 1, halo_n, pltpu.roll(zc, tm - 1, 0))
    cw = cw_ref[...]
    conv = prev * cw[0:1, :] + zc * cw[1:2, :] + nxt * cw[2:3, :]
    ya_ref[...] = (gb_ref[...].astype(_F32) * conv).astype(ya_ref.dtype)

    v = _gelu(v_ref[...].astype(_F32))
    mu = jnp.mean(v, axis=-1, keepdims=True)
    vc = v - mu
    var = jnp.mean(vc * vc, axis=-1, keepdims=True)
    vn = (vc * lax.rsqrt(var + EPS) * lng_ref[...] + lnb_ref[...]).astype(_BF16)
    gd = width // groups
    for c in range(tm // chunk):
        r = slice(c * chunk, (c + 1) * chunk)
        for g in range(groups):
            cols = slice(g * gd, (g + 1) * gd)
            mixed = jnp.dot(sw_ref[g], vn[r, cols], preferred_element_type=_F32)
            mixed = mixed + sb_ref[:, cols]
            u = _gelu(u_ref[r, cols].astype(_F32))
            yb_ref[r, cols] = (u * mixed).astype(yb_ref.dtype)


def _mixer(z, conv_w, ln_g, ln_b, sg_w, sg_b, seq_len):
    t = z.shape[0]
    width = conv_w.shape[1]
    groups, chunk, _ = sg_w.shape
    assert ln_g.shape[0] == width, "conv and spatial-gating widths must match"
    tm = _blk(seq_len, 2 * chunk)
    assert tm % chunk == 0 and tm % BF16_SUBLANES == 0
    hb = tm // BF16_SUBLANES
    last_hb = t // BF16_SUBLANES - 1
    gd = width // groups
    sb_full = jnp.repeat(sg_b.T, gd, axis=1)

    def col(c):
        return pl.BlockSpec((tm, width), lambda i: (i, c))

    def halo_prev(c):
        return pl.BlockSpec((BF16_SUBLANES, width),
                            lambda i: (jnp.maximum(i * hb - 1, 0), c))

    def halo_next(c):
        return pl.BlockSpec((BF16_SUBLANES, width),
                            lambda i: (jnp.minimum((i + 1) * hb, last_hb), c))

    def full(shape):
        return pl.BlockSpec(shape, lambda i: (0,) * len(shape))

    kern = functools.partial(_mixer_kernel, seq_len=seq_len, chunk=chunk, groups=groups)
    return pl.pallas_call(
        kern,
        grid=(t // tm,),
        in_specs=[col(0), col(1), col(2), col(3), col(4),
                  halo_prev(0), halo_prev(2), halo_next(0), halo_next(2),
                  full((3, width)), full((1, width)), full((1, width)),
                  full((groups, chunk, chunk)), full((chunk, width))],
        out_specs=[pl.BlockSpec((tm, width), lambda i: (i, 0)),
                   pl.BlockSpec((tm, width), lambda i: (i, 0))],
        out_shape=[jax.ShapeDtypeStruct((t, width), _BF16),
                   jax.ShapeDtypeStruct((t, width), _BF16)],
        compiler_params=_params("parallel"),
        name="mixer",
    )(z, z, z, z, z, z, z, z, z,
      conv_w, ln_g.reshape(1, width), ln_b.reshape(1, width),
      sg_w.astype(_BF16), sb_full)


def _top_rows(sc, k):
    n = sc.shape[0]
    rows = lax.broadcasted_iota(jnp.int32, sc.shape, 0)
    vals, idxs = [], []
    for _ in range(k):
        m = jnp.max(sc, axis=0, keepdims=True)
        idx = jnp.min(jnp.where(sc == m, rows, n), axis=0, keepdims=True)
        sc = jnp.where(rows == idx, -jnp.inf, sc)
        vals.append(m)
        idxs.append(idx)
    return vals, idxs


def _candidate_groups(k):
    return [(k1, k // (k1 + 1)) for k1 in range(k)]


def _retrieve_kernel(q_ref, keys_ref, i1_ref, i2_ref, g_ref):
    hk = keys_ref.shape[2]
    k = PEER_TOPK
    tops = []
    for s in range(2):
        sc = lax.dot_general(keys_ref[s], q_ref[:, s * hk:(s + 1) * hk],
                             (((1,), (1,)), ((), ())),
                             precision=lax.Precision.HIGHEST,
                             preferred_element_type=_F32)
        tops.append(_top_rows(sc, k))
    (s1, i1), (s2, i2) = tops
    cand, c1, c2 = [], [], []
    for k1, n2 in _candidate_groups(k):
        for k2 in range(n2):
            cand.append(s1[k1] + s2[k2])
            c1.append(i1[k1])
            c2.append(i2[k2])
    n_c = len(cand)
    pad = (-n_c) % 8
    lanes = cand[0].shape[1]
    cand = jnp.concatenate(cand + [jnp.full((pad, lanes), -jnp.inf, _F32)], axis=0)
    c1 = jnp.concatenate(c1 + [jnp.zeros((pad, lanes), jnp.int32)], axis=0)
    c2 = jnp.concatenate(c2 + [jnp.zeros((pad, lanes), jnp.int32)], axis=0)
    rows = lax.broadcasted_iota(jnp.int32, cand.shape, 0)
    top, e1, e2 = [], [], []
    for _ in range(k):
        m = jnp.max(cand, axis=0, keepdims=True)
        idx = jnp.min(jnp.where(cand == m, rows, n_c + pad), axis=0, keepdims=True)
        sel = rows == idx
        e1.append(jnp.sum(jnp.where(sel, c1, 0), axis=0, keepdims=True))
        e2.append(jnp.sum(jnp.where(sel, c2, 0), axis=0, keepdims=True))
        cand = jnp.where(sel, -jnp.inf, cand)
        top.append(m)
    top = jnp.concatenate(top, axis=0)
    ex = jnp.exp(top - top[0:1, :])
    g_ref[...] = ex / jnp.sum(ex, axis=0, keepdims=True)
    i1_ref[...] = jnp.concatenate(e1, axis=0).astype(_F32)
    i2_ref[...] = jnp.concatenate(e2, axis=0).astype(_F32)


def _retrieve(q, keys):
    t = q.shape[0]
    heads, _, n_keys, hk = keys.shape
    tm = _blk(t, 256)
    k = PEER_TOPK
    out = jax.ShapeDtypeStruct((heads * k, t), _F32)
    ospec = pl.BlockSpec((k, tm), lambda i, h: (h, i))
    return pl.pallas_call(
        _retrieve_kernel,
        grid=(t // tm, heads),
        in_specs=[pl.BlockSpec((tm, 2 * hk), lambda i, h: (i, h)),
                  pl.BlockSpec((2, n_keys, hk), lambda i, h: (h, 0, 0))],
        out_specs=[ospec, ospec, ospec],
        out_shape=[out, out, out],
        compiler_params=_params("parallel", "parallel"),
        name="peer_retrieve",
    )(q, keys.reshape(heads * 2, n_keys, hk))


def _gate_matrix_kernel(i1_ref, i2_ref, g_ref, w_ref, s_ref, *, n_keys):
    tm = w_ref.shape[0]
    i1 = i1_ref[...].T
    i2 = i2_ref[...].T
    gt = g_ref[...].T
    slots = i1.shape[1]
    key_ids = lax.broadcasted_iota(jnp.int32, (n_keys, slots), 0).astype(_F32)
    for t in range(tm):
        a_hot = jnp.where(key_ids == i1[t:t + 1, :], 1.0, 0.0).astype(_BF16)
        b_val = jnp.where(key_ids == i2[t:t + 1, :], gt[t:t + 1, :], 0.0)
        b_hi = b_val.astype(_BF16)
        b_lo = (b_val - b_hi.astype(_F32)).astype(_BF16)
        dims = (((1,), (1,)), ((), ()))
        w_t = (lax.dot_general(a_hot, b_hi, dims, preferred_element_type=_F32)
               + lax.dot_general(a_hot, b_lo, dims, preferred_element_type=_F32))
        s_ref[t * W_PITCH:t * W_PITCH + n_keys, :] = w_t
    for a in range(n_keys):
        w_ref[:, a * n_keys:(a + 1) * n_keys] = (
            s_ref[pl.ds(a, tm, stride=W_PITCH), :].astype(w_ref.dtype))


def _gate_matrix(i1, i2, gates, n_keys):
    slots, t = i1.shape
    assert n_keys == LANES
    tm = _blk(t, LANES)
    spec = pl.BlockSpec((slots, tm), lambda i: (0, i))
    return pl.pallas_call(
        functools.partial(_gate_matrix_kernel, n_keys=n_keys),
        grid=(t // tm,),
        in_specs=[spec, spec, spec],
        out_specs=pl.BlockSpec((tm, n_keys * n_keys), lambda i: (i, 0)),
        out_shape=jax.ShapeDtypeStruct((t, n_keys * n_keys), _BF16),
        scratch_shapes=[pltpu.VMEM((tm * W_PITCH, LANES), _F32)],
        compiler_params=_params("parallel"),
        name="peer_gate_matrix",
    )(i1, i2, gates)


def _experts_kernel(h_ref, ut_ref, v_ref, w_ref, o_ref):
    n = pl.program_id(1)
    pre = jnp.dot(h_ref[...], ut_ref[...], preferred_element_type=_F32)
    act = (w_ref[...].astype(_F32) * _gelu(pre)).astype(_BF16)
    part = jnp.dot(act, v_ref[...], preferred_element_type=_F32)

    @pl.when(n == 0)
    def _():
        o_ref[...] = part

    @pl.when(n != 0)
    def _():
        o_ref[...] += part


def _experts(h, ut, v, w, bm=512, bn=512):
    t, d = h.shape
    e = ut.shape[1]
    bm, bn = _blk(t, bm), _blk(e, bn)
    return pl.pallas_call(
        _experts_kernel,
        grid=(t // bm, e // bn),
        in_specs=[pl.BlockSpec((bm, d), lambda i, n: (i, 0)),
                  pl.BlockSpec((d, bn), lambda i, n: (0, n)),
                  pl.BlockSpec((bn, d), lambda i, n: (n, 0)),
                  pl.BlockSpec((bm, bn), lambda i, n: (i, n))],
        out_specs=pl.BlockSpec((bm, d), lambda i, n: (i, 0)),
        out_shape=jax.ShapeDtypeStruct((t, d), _F32),
        compiler_params=_params("parallel", "arbitrary"),
        name="peer_experts",
    )(h, ut, v, w)


def _trunk(x3, p):
    bt, seq_len, d = x3.shape
    x = x3.reshape(bt * seq_len, d)
    depth = p["w_in"].shape[0]
    width = p["conv_w"].shape[-1]
    n_keys = p["peer_keys"].shape[-2]
    h = _rmsnorm(x, p["norm1_g"][0], _BF16)
    for l in range(depth):
        z = _mm(h, p["w_in"][l], _BF16)
        ya, yb = _mixer(z, p["conv_w"][l], p["sg_ln_g"][l], p["sg_ln_b"][l],
                        p["sg_w"][l], p["sg_b"][l], seq_len)
        merged = _merge(ya, yb, p["proj_a"][l], p["proj_b"][l], z, 5 * width)
        x = _mm_res(merged, p["w_o"][l], x)
        h = _rmsnorm(x, p["norm2_g"][l], _BF16)
        q = _mm(h, p["peer_wq"][l], _F32)
        i1, i2, gates = _retrieve(q, p["peer_keys"][l])
        w = _gate_matrix(i1, i2, gates, n_keys)
        delta = _experts(h, p["peer_ut"][l], p["peer_v"][l], w)
        if l + 1 < depth:
            x, h = _add_rmsnorm(x, delta, p["norm1_g"][l + 1], _BF16)
        else:
            _, y = _add_rmsnorm(x, delta, p["final_g"], _F32)
    return y.reshape(bt, seq_len, d)


def kernel(x_prompt, x_sample, norm1_g, w_in, conv_w, sg_ln_g, sg_ln_b, sg_w, sg_b, proj_a, proj_b, w_o, norm2_g, peer_wq, peer_keys, peer_u, peer_v, final_g):
    p = dict(
        norm1_g=norm1_g, conv_w=conv_w, sg_ln_g=sg_ln_g, sg_ln_b=sg_ln_b, sg_w=sg_w, sg_b=sg_b,
        norm2_g=norm2_g, peer_keys=peer_keys, final_g=final_g,
        w_in=w_in.astype(_BF16), proj_a=proj_a.astype(_BF16), proj_b=proj_b.astype(_BF16),
        w_o=w_o.astype(_BF16), peer_wq=peer_wq.astype(_BF16),
        peer_ut=jnp.swapaxes(peer_u, 1, 2).astype(_BF16), peer_v=peer_v.astype(_BF16),
    )
    return (_trunk(x_prompt, p), _trunk(x_sample, p))
```

```python
import functools
import math

import jax
import jax.numpy as jnp
from jax import lax
from jax.experimental import pallas as pl
from jax.experimental.pallas import tpu as pltpu

PEER_TOPK = 16
EPS = 1e-6
V7X_VMEM_LIMIT_BYTES = 56 * 1024 * 1024
BF16_SUBLANES = 16
LANES = 128
W_PITCH = 136

_F32 = jnp.float32
_BF16 = jnp.bfloat16


def _params(*semantics):
    return pltpu.CompilerParams(dimension_semantics=semantics,
                                vmem_limit_bytes=V7X_VMEM_LIMIT_BYTES)


def _blk(n, target):
    if n <= target:
        return n
    for b in range(target - target % LANES, 0, -LANES):
        if n % b == 0:
            return b
    raise ValueError(f"no lane-aligned block for {n} <= {target}")


def _gelu(x):
    return 0.5 * x * (1.0 + lax.erf(x * (1.0 / math.sqrt(2.0))))


def _rmsnorm_kernel(x_ref, g_ref, h_ref):
    x = x_ref[...]
    ms = jnp.mean(x * x, axis=-1, keepdims=True)
    h_ref[...] = (x * lax.rsqrt(ms + EPS) * g_ref[...]).astype(h_ref.dtype)


def _rmsnorm(x, g, out_dtype):
    t, d = x.shape
    bm = _blk(t, 512)
    return pl.pallas_call(
        _rmsnorm_kernel,
        grid=(t // bm,),
        in_specs=[pl.BlockSpec((bm, d), lambda i: (i, 0)),
                  pl.BlockSpec((1, d), lambda i: (0, 0))],
        out_specs=pl.BlockSpec((bm, d), lambda i: (i, 0)),
        out_shape=jax.ShapeDtypeStruct((t, d), out_dtype),
        compiler_params=_params("parallel"),
        name="rmsnorm",
    )(x, g.reshape(1, d))


def _add_rmsnorm_kernel(x_ref, d_ref, g_ref, xo_ref, h_ref):
    x = x_ref[...] + d_ref[...]
    xo_ref[...] = x
    ms = jnp.mean(x * x, axis=-1, keepdims=True)
    h_ref[...] = (x * lax.rsqrt(ms + EPS) * g_ref[...]).astype(h_ref.dtype)


def _add_rmsnorm(x, delta, g, out_dtype):
    t, d = x.shape
    bm = _blk(t, 256)
    return pl.pallas_call(
        _add_rmsnorm_kernel,
        grid=(t // bm,),
        in_specs=[pl.BlockSpec((bm, d), lambda i: (i, 0)),
                  pl.BlockSpec((bm, d), lambda i: (i, 0)),
                  pl.BlockSpec((1, d), lambda i: (0, 0))],
        out_specs=[pl.BlockSpec((bm, d), lambda i: (i, 0)),
                   pl.BlockSpec((bm, d), lambda i: (i, 0))],
        out_shape=[jax.ShapeDtypeStruct((t, d), _F32),
                   jax.ShapeDtypeStruct((t, d), out_dtype)],
        compiler_params=_params("parallel"),
        name="add_rmsnorm",
    )(x, delta, g.reshape(1, d))


def _mm_kernel(x_ref, w_ref, o_ref):
    o_ref[...] = jnp.dot(x_ref[...], w_ref[...],
                         preferred_element_type=_F32).astype(o_ref.dtype)


def _mm(x, w, l, out_dtype, bm=1024, bn=1024):
    m, k = x.shape
    n = w.shape[2]
    bm, bn = _blk(m, bm), _blk(n, bn)
    return pl.pallas_call(
        _mm_kernel,
        grid=(m // bm, n // bn),
        in_specs=[pl.BlockSpec((bm, k), lambda i, j: (i, 0)),
                  pl.BlockSpec((None, k, bn), lambda i, j: (l, 0, j))],
        out_specs=pl.BlockSpec((bm, bn), lambda i, j: (i, j)),
        out_shape=jax.ShapeDtypeStruct((m, n), out_dtype),
        compiler_params=_params("parallel", "parallel"),
        name="matmul",
    )(x, w)


def _mm_res_kernel(x_ref, w_ref, r_ref, o_ref):
    o_ref[...] = r_ref[...] + jnp.dot(x_ref[...], w_ref[...], preferred_element_type=_F32)


def _mm_res(x, w, l, res, bm=1024, bn=1024):
    m, k = x.shape
    n = w.shape[2]
    bm, bn = _blk(m, bm), _blk(n, bn)
    return pl.pallas_call(
        _mm_res_kernel,
        grid=(m // bm, n // bn),
        in_specs=[pl.BlockSpec((bm, k), lambda i, j: (i, 0)),
                  pl.BlockSpec((None, k, bn), lambda i, j: (l, 0, j)),
                  pl.BlockSpec((bm, bn), lambda i, j: (i, j))],
        out_specs=pl.BlockSpec((bm, bn), lambda i, j: (i, j)),
        out_shape=jax.ShapeDtypeStruct((m, n), _F32),
        compiler_params=_params("parallel", "parallel"),
        name="matmul_residual",
    )(x, w, res)


def _merge_kernel(ya_ref, yb_ref, pa_ref, pb_ref, ga_ref, gb_ref, o_ref):
    a = jnp.dot(ya_ref[...], pa_ref[...], preferred_element_type=_F32)
    b = jnp.dot(yb_ref[...], pb_ref[...], preferred_element_type=_F32)
    ga = jax.nn.sigmoid(ga_ref[...].astype(_F32))
    gb = jax.nn.sigmoid(gb_ref[...].astype(_F32))
    o_ref[...] = (ga * a + gb * b).astype(o_ref.dtype)


def _merge(ya, yb, pa, pb, l, z, gate_col0, bm=1024, bn=1024):
    m, k = ya.shape
    n = pa.shape[2]
    bm, bn = _blk(m, bm), _blk(math.gcd(n, gate_col0), bn)
    ga0 = gate_col0 // bn
    gb0 = (gate_col0 + n) // bn
    return pl.pallas_call(
        _merge_kernel,
        grid=(m // bm, n // bn),
        in_specs=[pl.BlockSpec((bm, k), lambda i, j: (i, 0)),
                  pl.BlockSpec((bm, k), lambda i, j: (i, 0)),
                  pl.BlockSpec((None, k, bn), lambda i, j: (l, 0, j)),
                  pl.BlockSpec((None, k, bn), lambda i, j: (l, 0, j)),
                  pl.BlockSpec((bm, bn), lambda i, j: (i, ga0 + j)),
                  pl.BlockSpec((bm, bn), lambda i, j: (i, gb0 + j))],
        out_specs=pl.BlockSpec((bm, bn), lambda i, j: (i, j)),
        out_shape=jax.ShapeDtypeStruct((m, n), _BF16),
        compiler_params=_params("parallel", "parallel"),
        name="merge_matmul",
    )(ya, yb, pa, pb, z, z)


def _mixer_kernel(xin_ref, gb_ref, gc_ref, u_ref, v_ref,
                  xin_p_ref, gc_p_ref, xin_n_ref, gc_n_ref,
                  cw_ref, lng_ref, lnb_ref, sw_ref, sb_ref,
                  ya_ref, yb_ref, *, seq_len, chunk, groups):
    i = pl.program_id(0)
    tm, width = xin_ref.shape
    zc = gc_ref[...].astype(_F32) * xin_ref[...].astype(_F32)
    zp = gc_p_ref[...].astype(_F32) * xin_p_ref[...].astype(_F32)
    zn = gc_n_ref[...].astype(_F32) * xin_n_ref[...].astype(_F32)
    row0 = i * tm
    has_prev = (row0 % seq_len != 0).astype(_F32)
    has_next = ((row0 + tm) % seq_len != 0).astype(_F32)
    halo_p = zp[BF16_SUBLANES - 1:BF16_SUBLANES, :] * has_prev
    halo_n = zn[0:1, :] * has_next
    rows = lax.broadcasted_iota(jnp.int32, (tm, width), 0)
    prev = jnp.where(rows == 0, halo_p, pltpu.roll(zc, 1, 0))
    nxt = jnp.where(rows == tm - 1, halo_n, pltpu.roll(zc, tm - 1, 0))
    cw = cw_ref[...]
    conv = prev * cw[0:1, :] + zc * cw[1:2, :] + nxt * cw[2:3, :]
    ya_ref[...] = (gb_ref[...].astype(_F32) * conv).astype(ya_ref.dtype)

    v = _gelu(v_ref[...].astype(_F32))
    mu = jnp.mean(v, axis=-1, keepdims=True)
    vc = v - mu
    var = jnp.mean(vc * vc, axis=-1, keepdims=True)
    vn = (vc * lax.rsqrt(var + EPS) * lng_ref[...] + lnb_ref[...]).astype(_BF16)
    gd = width // groups
    for c in range(tm // chunk):
        r = slice(c * chunk, (c + 1) * chunk)
        for g in range(groups):
            cols = slice(g * gd, (g + 1) * gd)
            mixed = jnp.dot(sw_ref[g], vn[r, cols], preferred_element_type=_F32)
            mixed = mixed + sb_ref[:, cols]
            u = _gelu(u_ref[r, cols].astype(_F32))
            yb_ref[r, cols] = (u * mixed).astype(yb_ref.dtype)


def _mixer(z, conv_w, ln_g, ln_b, sg_w, sg_b, seq_len):
    t = z.shape[0]
    width = conv_w.shape[1]
    groups, chunk, _ = sg_w.shape
    assert ln_g.shape[0] == width, "conv and spatial-gating widths must match"
    tm = _blk(seq_len, 2 * chunk)
    assert tm % chunk == 0 and tm % BF16_SUBLANES == 0
    hb = tm // BF16_SUBLANES
    last_hb = t // BF16_SUBLANES - 1
    gd = width // groups
    sb_full = jnp.repeat(sg_b.T, gd, axis=1)

    def col(c):
        return pl.BlockSpec((tm, width), lambda i: (i, c))

    def halo_prev(c):
        return pl.BlockSpec((BF16_SUBLANES, width),
                            lambda i: (jnp.maximum(i * hb - 1, 0), c))

    def halo_next(c):
        return pl.BlockSpec((BF16_SUBLANES, width),
                            lambda i: (jnp.minimum((i + 1) * hb, last_hb), c))

    def full(shape):
        return pl.BlockSpec(shape, lambda i: (0,) * len(shape))

    kern = functools.partial(_mixer_kernel, seq_len=seq_len, chunk=chunk, groups=groups)
    return pl.pallas_call(
        kern,
        grid=(t // tm,),
        in_specs=[col(0), col(1), col(2), col(3), col(4),
                  halo_prev(0), halo_prev(2), halo_next(0), halo_next(2),
                  full((3, width)), full((1, width)), full((1, width)),
                  full((groups, chunk, chunk)), full((chunk, width))],
        out_specs=[pl.BlockSpec((tm, width), lambda i: (i, 0)),
                   pl.BlockSpec((tm, width), lambda i: (i, 0))],
        out_shape=[jax.ShapeDtypeStruct((t, width), _BF16),
                   jax.ShapeDtypeStruct((t, width), _BF16)],
        compiler_params=_params("parallel"),
        name="mixer",
    )(z, z, z, z, z, z, z, z, z,
      conv_w, ln_g.reshape(1, width), ln_b.reshape(1, width),
      sg_w.astype(_BF16), sb_full)


def _top_rows(sc, k):
    n = sc.shape[0]
    rows = lax.broadcasted_iota(jnp.int32, sc.shape, 0)
    vals, idxs = [], []
    for _ in range(k):
        m = jnp.max(sc, axis=0, keepdims=True)
        idx = jnp.min(jnp.where(sc == m, rows, n), axis=0, keepdims=True)
        sc = jnp.where(rows == idx, -jnp.inf, sc)
        vals.append(m)
        idxs.append(idx)
    return vals, idxs


def _candidate_groups(k):
    return [(k1, k // (k1 + 1)) for k1 in range(k)]


def _retrieve_kernel(q_ref, keys_ref, i1_ref, i2_ref, g_ref):
    hk = keys_ref.shape[2]
    k = PEER_TOPK
    tops = []
    for s in range(2):
        sc = lax.dot_general(keys_ref[s], q_ref[:, s * hk:(s + 1) * hk],
                             (((1,), (1,)), ((), ())),
                             precision=lax.Precision.HIGHEST,
                             preferred_element_type=_F32)
        tops.append(_top_rows(sc, k))
    (s1, i1), (s2, i2) = tops
    cand, c1, c2 = [], [], []
    for k1, n2 in _candidate_groups(k):
        for k2 in range(n2):
            cand.append(s1[k1] + s2[k2])
            c1.append(i1[k1])
            c2.append(i2[k2])
    n_c = len(cand)
    pad = (-n_c) % 8
    lanes = cand[0].shape[1]
    cand = jnp.concatenate(cand + [jnp.full((pad, lanes), -jnp.inf, _F32)], axis=0)
    c1 = jnp.concatenate(c1 + [jnp.zeros((pad, lanes), jnp.int32)], axis=0)
    c2 = jnp.concatenate(c2 + [jnp.zeros((pad, lanes), jnp.int32)], axis=0)
    rows = lax.broadcasted_iota(jnp.int32, cand.shape, 0)
    top, e1, e2 = [], [], []
    for _ in range(k):
        m = jnp.max(cand, axis=0, keepdims=True)
        idx = jnp.min(jnp.where(cand == m, rows, n_c + pad), axis=0, keepdims=True)
        sel = rows == idx
        e1.append(jnp.sum(jnp.where(sel, c1, 0), axis=0, keepdims=True))
        e2.append(jnp.sum(jnp.where(sel, c2, 0), axis=0, keepdims=True))
        cand = jnp.where(sel, -jnp.inf, cand)
        top.append(m)
    top = jnp.concatenate(top, axis=0)
    ex = jnp.exp(top - top[0:1, :])
    g_ref[...] = ex / jnp.sum(ex, axis=0, keepdims=True)
    i1_ref[...] = jnp.concatenate(e1, axis=0).astype(_F32)
    i2_ref[...] = jnp.concatenate(e2, axis=0).astype(_F32)


def _retrieve(q, keys):
    t = q.shape[0]
    heads, _, n_keys, hk = keys.shape
    tm = _blk(t, 256)
    k = PEER_TOPK
    out = jax.ShapeDtypeStruct((heads * k, t), _F32)
    ospec = pl.BlockSpec((k, tm), lambda i, h: (h, i))
    return pl.pallas_call(
        _retrieve_kernel,
        grid=(t // tm, heads),
        in_specs=[pl.BlockSpec((tm, 2 * hk), lambda i, h: (i, h)),
                  pl.BlockSpec((2, n_keys, hk), lambda i, h: (h, 0, 0))],
        out_specs=[ospec, ospec, ospec],
        out_shape=[out, out, out],
        compiler_params=_params("parallel", "parallel"),
        name="peer_retrieve",
    )(q, keys.reshape(heads * 2, n_keys, hk))


def _gate_matrix_kernel(i1_ref, i2_ref, g_ref, w_ref, s_ref, *, n_keys):
    tm = w_ref.shape[0]
    i1 = i1_ref[...].T
    i2 = i2_ref[...].T
    gt = g_ref[...].T
    slots = i1.shape[1]
    key_ids = lax.broadcasted_iota(jnp.int32, (n_keys, slots), 0).astype(_F32).astype(_BF16)
    one = jnp.ones((n_keys, slots), _BF16)
    zero = jnp.zeros((n_keys, slots), _BF16)
    for t in range(tm):
        r1 = jnp.broadcast_to(i1[t:t + 1, :].astype(_BF16), (n_keys, slots))
        r2 = jnp.broadcast_to(i2[t:t + 1, :].astype(_BF16), (n_keys, slots))
        rg = jnp.broadcast_to(gt[t:t + 1, :].astype(_BF16), (n_keys, slots))
        a_hot = jnp.where(key_ids == r1, one, zero)
        b_val = jnp.where(key_ids == r2, rg, zero)
        w_t = lax.dot_general(a_hot, b_val, (((1,), (1,)), ((), ())),
                              preferred_element_type=_F32)
        s_ref[t * W_PITCH:t * W_PITCH + n_keys, :] = w_t
    for a in range(n_keys):
        w_ref[:, a * n_keys:(a + 1) * n_keys] = (
            s_ref[pl.ds(a, tm, stride=W_PITCH), :].astype(w_ref.dtype))


def _gate_matrix(i1, i2, gates, n_keys):
    slots, t = i1.shape
    assert n_keys == LANES
    tm = _blk(t, LANES)
    spec = pl.BlockSpec((slots, tm), lambda i: (0, i))
    return pl.pallas_call(
        functools.partial(_gate_matrix_kernel, n_keys=n_keys),
        grid=(t // tm,),
        in_specs=[spec, spec, spec],
        out_specs=pl.BlockSpec((tm, n_keys * n_keys), lambda i: (i, 0)),
        out_shape=jax.ShapeDtypeStruct((t, n_keys * n_keys), _BF16),
        scratch_shapes=[pltpu.VMEM((tm * W_PITCH, LANES), _F32)],
        compiler_params=_params("parallel"),
        name="peer_gate_matrix",
    )(i1, i2, gates)


def _experts_kernel(h_ref, ut_ref, v_ref, w_ref, o_ref):
    n = pl.program_id(1)
    pre = jnp.dot(h_ref[...], ut_ref[...], preferred_element_type=_F32)
    act = (w_ref[...].astype(_F32) * _gelu(pre)).astype(_BF16)

    @pl.when(n == 0)
    def _():
        o_ref[...] = jnp.zeros_like(o_ref)

    o_ref[...] += jnp.dot(act, v_ref[...], preferred_element_type=_F32)


def _experts(h, ut, v, l, w, bm=512, bn=1024):
    t, d = h.shape
    e = ut.shape[2]
    bm, bn = _blk(t, bm), _blk(e, bn)
    once = pl.Buffered(1)
    return pl.pallas_call(
        _experts_kernel,
        grid=(t // bm, e // bn),
        in_specs=[pl.BlockSpec((bm, d), lambda i, n: (i, 0), pipeline_mode=once),
                  pl.BlockSpec((None, d, bn), lambda i, n: (l, 0, n)),
                  pl.BlockSpec((None, bn, d), lambda i, n: (l, n, 0)),
                  pl.BlockSpec((bm, bn), lambda i, n: (i, n))],
        out_specs=pl.BlockSpec((bm, d), lambda i, n: (i, 0), pipeline_mode=once),
        out_shape=jax.ShapeDtypeStruct((t, d), _F32),
        compiler_params=_params("parallel", "arbitrary"),
        name="peer_experts",
    )(h, ut, v, w)


def _trunk(x3, p):
    bt, seq_len, d = x3.shape
    x = x3.reshape(bt * seq_len, d)
    depth = p["w_in"].shape[0]
    width = p["conv_w"].shape[-1]
    n_keys = p["peer_keys"].shape[-2]
    h = _rmsnorm(x, p["norm1_g"][0], _BF16)
    for l in range(depth):
        z = _mm(h, p["w_in"], l, _BF16)
        ya, yb = _mixer(z, p["conv_w"][l], p["sg_ln_g"][l], p["sg_ln_b"][l],
                        p["sg_w"][l], p["sg_b"][l], seq_len)
        merged = _merge(ya, yb, p["proj_a"], p["proj_b"], l, z, 5 * width)
        x = _mm_res(merged, p["w_o"], l, x)
        h = _rmsnorm(x, p["norm2_g"][l], _BF16)
        q = _mm(h, p["peer_wq"], l, _F32)
        i1, i2, gates = _retrieve(q, p["peer_keys"][l])
        w = _gate_matrix(i1, i2, gates, n_keys)
        delta = _experts(h, p["peer_ut"], p["peer_v"], l, w)
        if l + 1 < depth:
            x, h = _add_rmsnorm(x, delta, p["norm1_g"][l + 1], _BF16)
        else:
            _, y = _add_rmsnorm(x, delta, p["final_g"], _F32)
    return y.reshape(bt, seq_len, d)


def kernel(x_prompt, x_sample, norm1_g, w_in, conv_w, sg_ln_g, sg_ln_b, sg_w, sg_b, proj_a, proj_b, w_o, norm2_g, peer_wq, peer_keys, peer_u, peer_v, final_g):
    p = dict(
        norm1_g=norm1_g, conv_w=conv_w, sg_ln_g=sg_ln_g, sg_ln_b=sg_ln_b, sg_w=sg_w, sg_b=sg_b,
        norm2_g=norm2_g, peer_keys=peer_keys, final_g=final_g,
        w_in=w_in.astype(_BF16), proj_a=proj_a.astype(_BF16), proj_b=proj_b.astype(_BF16),
        w_o=w_o.astype(_BF16), peer_wq=peer_wq.astype(_BF16),
        peer_ut=jnp.swapaxes(peer_u, 1, 2).astype(_BF16), peer_v=peer_v.astype(_BF16),
    )
    return (_trunk(x_prompt, p), _trunk(x_sample, p))
```

```python
import functools
import math

import jax
import jax.numpy as jnp
from jax import lax
from jax.experimental import pallas as pl
from jax.experimental.pallas import tpu as pltpu

PEER_TOPK = 16
EPS = 1e-6
V7X_VMEM_LIMIT_BYTES = 56 * 1024 * 1024
BF16_SUBLANES = 16
LANES = 128
W_PITCH = 136

_F32 = jnp.float32
_BF16 = jnp.bfloat16


def _params(*semantics):
    return pltpu.CompilerParams(dimension_semantics=semantics,
                                vmem_limit_bytes=V7X_VMEM_LIMIT_BYTES)


def _blk(n, target):
    if n <= target:
        return n
    for b in range(target - target % LANES, 0, -LANES):
        if n % b == 0:
            return b
    raise ValueError(f"no lane-aligned block for {n} <= {target}")


def _gelu(x):
    return 0.5 * x * (1.0 + lax.erf(x * (1.0 / math.sqrt(2.0))))


def _rmsnorm_kernel(x_ref, g_ref, h_ref):
    x = x_ref[...]
    ms = jnp.mean(x * x, axis=-1, keepdims=True)
    h_ref[...] = (x * lax.rsqrt(ms + EPS) * g_ref[...]).astype(h_ref.dtype)


def _rmsnorm(x, g, out_dtype):
    t, d = x.shape
    bm = _blk(t, 512)
    return pl.pallas_call(
        _rmsnorm_kernel,
        grid=(t // bm,),
        in_specs=[pl.BlockSpec((bm, d), lambda i: (i, 0)),
                  pl.BlockSpec((1, d), lambda i: (0, 0))],
        out_specs=pl.BlockSpec((bm, d), lambda i: (i, 0)),
        out_shape=jax.ShapeDtypeStruct((t, d), out_dtype),
        compiler_params=_params("parallel"),
        name="rmsnorm",
    )(x, g.reshape(1, d))


def _add_rmsnorm_kernel(x_ref, d_ref, g_ref, xo_ref, h_ref):
    x = x_ref[...] + d_ref[...]
    xo_ref[...] = x
    ms = jnp.mean(x * x, axis=-1, keepdims=True)
    h_ref[...] = (x * lax.rsqrt(ms + EPS) * g_ref[...]).astype(h_ref.dtype)


def _add_rmsnorm(x, delta, g, out_dtype):
    t, d = x.shape
    bm = _blk(t, 256)
    return pl.pallas_call(
        _add_rmsnorm_kernel,
        grid=(t // bm,),
        in_specs=[pl.BlockSpec((bm, d), lambda i: (i, 0)),
                  pl.BlockSpec((bm, d), lambda i: (i, 0)),
                  pl.BlockSpec((1, d), lambda i: (0, 0))],
        out_specs=[pl.BlockSpec((bm, d), lambda i: (i, 0)),
                   pl.BlockSpec((bm, d), lambda i: (i, 0))],
        out_shape=[jax.ShapeDtypeStruct((t, d), _F32),
                   jax.ShapeDtypeStruct((t, d), out_dtype)],
        compiler_params=_params("parallel"),
        name="add_rmsnorm",
    )(x, delta, g.reshape(1, d))


def _mm_kernel(x_ref, w_ref, o_ref):
    o_ref[...] = jnp.dot(x_ref[...], w_ref[...],
                         preferred_element_type=_F32).astype(o_ref.dtype)


def _mm(x, w, l, out_dtype, bm=1024, bn=1024):
    m, k = x.shape
    n = w.shape[2]
    bm, bn = _blk(m, bm), _blk(n, bn)
    return pl.pallas_call(
        _mm_kernel,
        grid=(m // bm, n // bn),
        in_specs=[pl.BlockSpec((bm, k), lambda i, j: (i, 0)),
                  pl.BlockSpec((None, k, bn), lambda i, j: (l, 0, j))],
        out_specs=pl.BlockSpec((bm, bn), lambda i, j: (i, j)),
        out_shape=jax.ShapeDtypeStruct((m, n), out_dtype),
        compiler_params=_params("parallel", "parallel"),
        name="matmul",
    )(x, w)


def _mm_res_kernel(x_ref, w_ref, r_ref, o_ref):
    o_ref[...] = r_ref[...] + jnp.dot(x_ref[...], w_ref[...], preferred_element_type=_F32)


def _mm_res(x, w, l, res, bm=1024, bn=1024):
    m, k = x.shape
    n = w.shape[2]
    bm, bn = _blk(m, bm), _blk(n, bn)
    return pl.pallas_call(
        _mm_res_kernel,
        grid=(m // bm, n // bn),
        in_specs=[pl.BlockSpec((bm, k), lambda i, j: (i, 0)),
                  pl.BlockSpec((None, k, bn), lambda i, j: (l, 0, j)),
                  pl.BlockSpec((bm, bn), lambda i, j: (i, j))],
        out_specs=pl.BlockSpec((bm, bn), lambda i, j: (i, j)),
        out_shape=jax.ShapeDtypeStruct((m, n), _F32),
        compiler_params=_params("parallel", "parallel"),
        name="matmul_residual",
    )(x, w, res)


def _merge_kernel(ya_ref, yb_ref, pa_ref, pb_ref, ga_ref, gb_ref, o_ref):
    a = jnp.dot(ya_ref[...], pa_ref[...], preferred_element_type=_F32)
    b = jnp.dot(yb_ref[...], pb_ref[...], preferred_element_type=_F32)
    ga = jax.nn.sigmoid(ga_ref[...].astype(_F32))
    gb = jax.nn.sigmoid(gb_ref[...].astype(_F32))
    o_ref[...] = (ga * a + gb * b).astype(o_ref.dtype)


def _merge(ya, yb, pa, pb, l, z, gate_col0, bm=1024, bn=1024):
    m, k = ya.shape
    n = pa.shape[2]
    bm, bn = _blk(m, bm), _blk(math.gcd(n, gate_col0), bn)
    ga0 = gate_col0 // bn
    gb0 = (gate_col0 + n) // bn
    return pl.pallas_call(
        _merge_kernel,
        grid=(m // bm, n // bn),
        in_specs=[pl.BlockSpec((bm, k), lambda i, j: (i, 0)),
                  pl.BlockSpec((bm, k), lambda i, j: (i, 0)),
                  pl.BlockSpec((None, k, bn), lambda i, j: (l, 0, j)),
                  pl.BlockSpec((None, k, bn), lambda i, j: (l, 0, j)),
                  pl.BlockSpec((bm, bn), lambda i, j: (i, ga0 + j)),
                  pl.BlockSpec((bm, bn), lambda i, j: (i, gb0 + j))],
        out_specs=pl.BlockSpec((bm, bn), lambda i, j: (i, j)),
        out_shape=jax.ShapeDtypeStruct((m, n), _BF16),
        compiler_params=_params("parallel", "parallel"),
        name="merge_matmul",
    )(ya, yb, pa, pb, z, z)


def _mixer_kernel(xin_ref, gb_ref, gc_ref, u_ref, v_ref,
                  xin_p_ref, gc_p_ref, xin_n_ref, gc_n_ref,
                  cw_ref, lng_ref, lnb_ref, sw_ref, sb_ref,
                  ya_ref, yb_ref, *, seq_len, chunk, groups):
    i = pl.program_id(0)
    tm, width = xin_ref.shape
    zc = gc_ref[...].astype(_F32) * xin_ref[...].astype(_F32)
    zp = gc_p_ref[...].astype(_F32) * xin_p_ref[...].astype(_F32)
    zn = gc_n_ref[...].astype(_F32) * xin_n_ref[...].astype(_F32)
    row0 = i * tm
    has_prev = (row0 % seq_len != 0).astype(_F32)
    has_next = ((row0 + tm) % seq_len != 0).astype(_F32)
    halo_p = zp[BF16_SUBLANES - 1:BF16_SUBLANES, :] * has_prev
    halo_n = zn[0:1, :] * has_next
    rows = lax.broadcasted_iota(jnp.int32, (tm, width), 0)
    prev = jnp.where(rows == 0, halo_p, pltpu.roll(zc, 1, 0))
    nxt = jnp.where(rows == tm - 1, halo_n, pltpu.roll(zc, tm - 1, 0))
    cw = cw_ref[...]
    conv = prev * cw[0:1, :] + zc * cw[1:2, :] + nxt * cw[2:3, :]
    ya_ref[...] = (gb_ref[...].astype(_F32) * conv).astype(ya_ref.dtype)

    v = _gelu(v_ref[...].astype(_F32))
    mu = jnp.mean(v, axis=-1, keepdims=True)
    vc = v - mu
    var = jnp.mean(vc * vc, axis=-1, keepdims=True)
    vn = (vc * lax.rsqrt(var + EPS) * lng_ref[...] + lnb_ref[...]).astype(_BF16)
    gd = width // groups
    for c in range(tm // chunk):
        r = slice(c * chunk, (c + 1) * chunk)
        for g in range(groups):
            cols = slice(g * gd, (g + 1) * gd)
            mixed = jnp.dot(sw_ref[g], vn[r, cols], preferred_element_type=_F32)
            mixed = mixed + sb_ref[:, cols]
            u = _gelu(u_ref[r, cols].astype(_F32))
            yb_ref[r, cols] = (u * mixed).astype(yb_ref.dtype)


def _mixer(z, conv_w, ln_g, ln_b, sg_w, sg_b, seq_len):
    t = z.shape[0]
    width = conv_w.shape[1]
    groups, chunk, _ = sg_w.shape
    assert ln_g.shape[0] == width, "conv and spatial-gating widths must match"
    tm = _blk(seq_len, 2 * chunk)
    assert tm % chunk == 0 and tm % BF16_SUBLANES == 0
    hb = tm // BF16_SUBLANES
    last_hb = t // BF16_SUBLANES - 1
    gd = width // groups
    sb_full = jnp.repeat(sg_b.T, gd, axis=1)

    def col(c):
        return pl.BlockSpec((tm, width), lambda i: (i, c))

    def halo_prev(c):
        return pl.BlockSpec((BF16_SUBLANES, width),
                            lambda i: (jnp.maximum(i * hb - 1, 0), c))

    def halo_next(c):
        return pl.BlockSpec((BF16_SUBLANES, width),
                            lambda i: (jnp.minimum((i + 1) * hb, last_hb), c))

    def full(shape):
        return pl.BlockSpec(shape, lambda i: (0,) * len(shape))

    kern = functools.partial(_mixer_kernel, seq_len=seq_len, chunk=chunk, groups=groups)
    return pl.pallas_call(
        kern,
        grid=(t // tm,),
        in_specs=[col(0), col(1), col(2), col(3), col(4),
                  halo_prev(0), halo_prev(2), halo_next(0), halo_next(2),
                  full((3, width)), full((1, width)), full((1, width)),
                  full((groups, chunk, chunk)), full((chunk, width))],
        out_specs=[pl.BlockSpec((tm, width), lambda i: (i, 0)),
                   pl.BlockSpec((tm, width), lambda i: (i, 0))],
        out_shape=[jax.ShapeDtypeStruct((t, width), _BF16),
                   jax.ShapeDtypeStruct((t, width), _BF16)],
        compiler_params=_params("parallel"),
        name="mixer",
    )(z, z, z, z, z, z, z, z, z,
      conv_w, ln_g.reshape(1, width), ln_b.reshape(1, width),
      sg_w.astype(_BF16), sb_full)


def _top_rows(sc, k):
    n = sc.shape[0]
    rows = lax.broadcasted_iota(jnp.int32, sc.shape, 0)
    vals, idxs = [], []
    for _ in range(k):
        m = jnp.max(sc, axis=0, keepdims=True)
        idx = jnp.min(jnp.where(sc == m, rows, n), axis=0, keepdims=True)
        sc = jnp.where(rows == idx, -jnp.inf, sc)
        vals.append(m)
        idxs.append(idx)
    return vals, idxs


def _candidate_groups(k):
    return [(k1, k // (k1 + 1)) for k1 in range(k)]


def _query_scores_kernel(h_ref, wq_ref, keys_ref, sc_ref):
    hk = keys_ref.shape[2]
    q = jnp.dot(h_ref[...], wq_ref[...], preferred_element_type=_F32)
    for hs in range(keys_ref.shape[0]):
        sc_ref[hs] = lax.dot_general(keys_ref[hs], q[:, hs * hk:(hs + 1) * hk],
                                     (((1,), (1,)), ((), ())),
                                     precision=lax.Precision.HIGHEST,
                                     preferred_element_type=_F32)


def _query_scores(h, wq, l, keys, bm=1024, bn=1024):
    t, d = h.shape
    heads, _, n_keys, hk = keys.shape
    n = heads * 2 * hk
    bm, bn = _blk(t, bm), _blk(n, bn)
    assert bn % hk == 0
    per = bn // hk
    return pl.pallas_call(
        _query_scores_kernel,
        grid=(t // bm, n // bn),
        in_specs=[pl.BlockSpec((bm, d), lambda i, j: (i, 0)),
                  pl.BlockSpec((None, d, bn), lambda i, j: (l, 0, j)),
                  pl.BlockSpec((per, n_keys, hk), lambda i, j: (j, 0, 0))],
        out_specs=pl.BlockSpec((per, n_keys, bm), lambda i, j: (j, 0, i)),
        out_shape=jax.ShapeDtypeStruct((heads * 2, n_keys, t), _F32),
        compiler_params=_params("parallel", "parallel"),
        name="query_scores",
    )(h, wq, keys.reshape(heads * 2, n_keys, hk))


def _retrieve_kernel(sc_ref, i1_ref, i2_ref, g_ref):
    k = PEER_TOPK
    (s1, i1), (s2, i2) = _top_rows(sc_ref[0], k), _top_rows(sc_ref[1], k)
    cand, c1, c2 = [], [], []
    for k1, n2 in _candidate_groups(k):
        for k2 in range(n2):
            cand.append(s1[k1] + s2[k2])
            c1.append(i1[k1])
            c2.append(i2[k2])
    n_c = len(cand)
    pad = (-n_c) % 8
    lanes = cand[0].shape[1]
    cand = jnp.concatenate(cand + [jnp.full((pad, lanes), -jnp.inf, _F32)], axis=0)
    c1 = jnp.concatenate(c1 + [jnp.zeros((pad, lanes), jnp.int32)], axis=0)
    c2 = jnp.concatenate(c2 + [jnp.zeros((pad, lanes), jnp.int32)], axis=0)
    rows = lax.broadcasted_iota(jnp.int32, cand.shape, 0)
    top, e1, e2 = [], [], []
    for _ in range(k):
        m = jnp.max(cand, axis=0, keepdims=True)
        idx = jnp.min(jnp.where(cand == m, rows, n_c + pad), axis=0, keepdims=True)
        sel = rows == idx
        e1.append(jnp.sum(jnp.where(sel, c1, 0), axis=0, keepdims=True))
        e2.append(jnp.sum(jnp.where(sel, c2, 0), axis=0, keepdims=True))
        cand = jnp.where(sel, -jnp.inf, cand)
        top.append(m)
    top = jnp.concatenate(top, axis=0)
    ex = jnp.exp(top - top[0:1, :])
    g_ref[...] = ex / jnp.sum(ex, axis=0, keepdims=True)
    i1_ref[...] = jnp.concatenate(e1, axis=0).astype(_F32)
    i2_ref[...] = jnp.concatenate(e2, axis=0).astype(_F32)


def _retrieve(sc):
    hs, n_keys, t = sc.shape
    heads = hs // 2
    tm = _blk(t, 256)
    k = PEER_TOPK
    out = jax.ShapeDtypeStruct((heads * k, t), _F32)
    ospec = pl.BlockSpec((k, tm), lambda i, h: (h, i))
    return pl.pallas_call(
        _retrieve_kernel,
        grid=(t // tm, heads),
        in_specs=[pl.BlockSpec((2, n_keys, tm), lambda i, h: (h, 0, i))],
        out_specs=[ospec, ospec, ospec],
        out_shape=[out, out, out],
        compiler_params=_params("parallel", "parallel"),
        name="peer_retrieve",
    )(sc)


def _gate_matrix_kernel(i1_ref, i2_ref, g_ref, w_ref, s_ref, *, n_keys):
    tm = w_ref.shape[0]
    i1 = i1_ref[...].T
    i2 = i2_ref[...].T
    gt = g_ref[...].T
    slots = i1.shape[1]
    key_ids = lax.broadcasted_iota(jnp.int32, (n_keys, slots), 0).astype(_F32).astype(_BF16)
    one = jnp.ones((n_keys, slots), _BF16)
    zero = jnp.zeros((n_keys, slots), _BF16)
    for t in range(tm):
        r1 = jnp.broadcast_to(i1[t:t + 1, :].astype(_BF16), (n_keys, slots))
        r2 = jnp.broadcast_to(i2[t:t + 1, :].astype(_BF16), (n_keys, slots))
        rg = jnp.broadcast_to(gt[t:t + 1, :].astype(_BF16), (n_keys, slots))
        a_hot = jnp.where(key_ids == r1, one, zero)
        b_val = jnp.where(key_ids == r2, rg, zero)
        w_t = lax.dot_general(a_hot, b_val, (((1,), (1,)), ((), ())),
                              preferred_element_type=_F32)
        s_ref[t * W_PITCH:t * W_PITCH + n_keys, :] = w_t
    for a in range(n_keys):
        w_ref[:, a * n_keys:(a + 1) * n_keys] = (
            s_ref[pl.ds(a, tm, stride=W_PITCH), :].astype(w_ref.dtype))


def _gate_matrix(i1, i2, gates, n_keys):
    slots, t = i1.shape
    assert n_keys == LANES
    tm = _blk(t, LANES)
    spec = pl.BlockSpec((slots, tm), lambda i: (0, i))
    return pl.pallas_call(
        functools.partial(_gate_matrix_kernel, n_keys=n_keys),
        grid=(t // tm,),
        in_specs=[spec, spec, spec],
        out_specs=pl.BlockSpec((tm, n_keys * n_keys), lambda i: (i, 0)),
        out_shape=jax.ShapeDtypeStruct((t, n_keys * n_keys), _BF16),
        scratch_shapes=[pltpu.VMEM((tm * W_PITCH, LANES), _F32)],
        compiler_params=_params("parallel"),
        name="peer_gate_matrix",
    )(i1, i2, gates)


def _experts_kernel(h_ref, ut_ref, v_ref, w_ref, o_ref):
    n = pl.program_id(1)
    pre = jnp.dot(h_ref[...], ut_ref[...], preferred_element_type=_F32)
    act = (w_ref[...].astype(_F32) * _gelu(pre)).astype(_BF16)

    @pl.when(n == 0)
    def _():
        o_ref[...] = jnp.zeros_like(o_ref)

    o_ref[...] += jnp.dot(act, v_ref[...], preferred_element_type=_F32)


def _experts_retrieve_kernel(h_ref, ut_ref, v_ref, w_ref, sc_ref,
                             o_ref, i1_ref, i2_ref, g_ref):
    _retrieve_kernel(sc_ref, i1_ref, i2_ref, g_ref)
    _experts_kernel(h_ref, ut_ref, v_ref, w_ref, o_ref)


def _experts(h, ut, v, l, w, guest=None, bm=512, bn=1024):
    t, d = h.shape
    e = ut.shape[2]
    bm, bn = _blk(t, bm), _blk(e, bn)
    steps_n = e // bn
    once = pl.Buffered(1)
    in_specs = [pl.BlockSpec((bm, d), lambda i, n: (i, 0), pipeline_mode=once),
                pl.BlockSpec((None, d, bn), lambda i, n: (l, 0, n)),
                pl.BlockSpec((None, bn, d), lambda i, n: (l, n, 0)),
                pl.BlockSpec((bm, bn), lambda i, n: (i, n))]
    out_spec = pl.BlockSpec((bm, d), lambda i, n: (i, 0), pipeline_mode=once)
    out_shape = jax.ShapeDtypeStruct((t, d), _F32)
    if guest is None:
        return pl.pallas_call(
            _experts_kernel,
            grid=(t // bm, steps_n),
            in_specs=in_specs,
            out_specs=out_spec,
            out_shape=out_shape,
            compiler_params=_params("parallel", "arbitrary"),
            name="peer_experts",
        )(h, ut, v, w), None

    hs, n_keys, tg = guest.shape
    heads = hs // 2
    k = PEER_TOPK
    steps = (t // bm) * steps_n
    assert (tg * heads) % steps == 0
    tmg = tg * heads // steps
    assert tmg % LANES == 0 and tg % tmg == 0

    def tile(i, n):
        return (i * steps_n + n) // heads

    def head(i, n):
        return (i * steps_n + n) % heads

    r_shape = jax.ShapeDtypeStruct((heads * k, tg), _F32)
    r_spec = pl.BlockSpec((k, tmg), lambda i, n: (head(i, n), tile(i, n)))
    delta, i1, i2, g = pl.pallas_call(
        _experts_retrieve_kernel,
        grid=(t // bm, steps_n),
        in_specs=in_specs + [
            pl.BlockSpec((2, n_keys, tmg), lambda i, n: (head(i, n), 0, tile(i, n)))],
        out_specs=[out_spec, r_spec, r_spec, r_spec],
        out_shape=[out_shape, r_shape, r_shape, r_shape],
        compiler_params=_params("parallel", "arbitrary"),
        name="peer_experts_retrieve",
    )(h, ut, v, w, guest)
    return delta, (i1, i2, g)


def _mixer_block(x, h, p, l, seq_len):
    width = p["conv_w"].shape[-1]
    z = _mm(h, p["w_in"], l, _BF16)
    ya, yb = _mixer(z, p["conv_w"][l], p["sg_ln_g"][l], p["sg_ln_b"][l],
                    p["sg_w"][l], p["sg_b"][l], seq_len)
    merged = _merge(ya, yb, p["proj_a"], p["proj_b"], l, z, 5 * width)
    x = _mm_res(merged, p["w_o"], l, x)
    h = _rmsnorm(x, p["norm2_g"][l], _BF16)
    sc = _query_scores(h, p["peer_wq"], l, p["peer_keys"][l])
    return x, h, sc


def _trunk_pair(xa3, xb3, p):
    depth = p["w_in"].shape[0]
    n_keys = p["peer_keys"].shape[-2]
    d = xa3.shape[-1]
    seq_a, seq_b = xa3.shape[1], xb3.shape[1]
    xa, xb = xa3.reshape(-1, d), xb3.reshape(-1, d)
    ha = _rmsnorm(xa, p["norm1_g"][0], _BF16)
    hb = _rmsnorm(xb, p["norm1_g"][0], _BF16)
    xa, ha, qa = _mixer_block(xa, ha, p, 0, seq_a)
    xb, hb, qb = _mixer_block(xb, hb, p, 0, seq_b)
    ra = _retrieve(qa)
    for l in range(depth):
        last = l + 1 == depth
        gain, out_dtype = (p["final_g"], _F32) if last else (p["norm1_g"][l + 1], _BF16)
        wa = _gate_matrix(*ra, n_keys)
        da, rb = _experts(ha, p["peer_ut"], p["peer_v"], l, wa, guest=qb)
        xa, ha = _add_rmsnorm(xa, da, gain, out_dtype)
        wb = _gate_matrix(*rb, n_keys)
        if last:
            db, _ = _experts(hb, p["peer_ut"], p["peer_v"], l, wb)
        else:
            xa, ha, qa = _mixer_block(xa, ha, p, l + 1, seq_a)
            db, ra = _experts(hb, p["peer_ut"], p["peer_v"], l, wb, guest=qa)
        xb, hb = _add_rmsnorm(xb, db, gain, out_dtype)
        if not last:
            xb, hb, qb = _mixer_block(xb, hb, p, l + 1, seq_b)
    return ha.reshape(xa3.shape), hb.reshape(xb3.shape)


def kernel(x_prompt, x_sample, norm1_g, w_in, conv_w, sg_ln_g, sg_ln_b, sg_w, sg_b, proj_a, proj_b, w_o, norm2_g, peer_wq, peer_keys, peer_u, peer_v, final_g):
    p = dict(
        norm1_g=norm1_g, conv_w=conv_w, sg_ln_g=sg_ln_g, sg_ln_b=sg_ln_b, sg_w=sg_w, sg_b=sg_b,
        norm2_g=norm2_g, peer_keys=peer_keys, final_g=final_g,
        w_in=w_in.astype(_BF16), proj_a=proj_a.astype(_BF16), proj_b=proj_b.astype(_BF16),
        w_o=w_o.astype(_BF16), peer_wq=peer_wq.astype(_BF16),
        peer_ut=jnp.swapaxes(peer_u, 1, 2).astype(_BF16), peer_v=peer_v.astype(_BF16),
    )
    return _trunk_pair(x_prompt, x_sample, p)
```

```python
import functools
import math

import jax
import jax.numpy as jnp
from jax import lax
from jax.experimental import pallas as pl
from jax.experimental.pallas import tpu as pltpu

PEER_TOPK = 16
EPS = 1e-6
V7X_VMEM_LIMIT_BYTES = 56 * 1024 * 1024
BF16_SUBLANES = 16
LANES = 128
W_PITCH = 136
RETRIEVE_CHUNK_LANES = 256

_F32 = jnp.float32
_BF16 = jnp.bfloat16


def _params(*semantics):
    return pltpu.CompilerParams(dimension_semantics=semantics,
                                vmem_limit_bytes=V7X_VMEM_LIMIT_BYTES)


def _blk(n, target):
    if n <= target:
        return n
    for b in range(target - target % LANES, 0, -LANES):
        if n % b == 0:
            return b
    raise ValueError(f"no lane-aligned block for {n} <= {target}")


def _gelu(x):
    return 0.5 * x * (1.0 + lax.erf(x * (1.0 / math.sqrt(2.0))))


def _rmsnorm_kernel(x_ref, g_ref, h_ref):
    x = x_ref[...]
    ms = jnp.mean(x * x, axis=-1, keepdims=True)
    h_ref[...] = (x * lax.rsqrt(ms + EPS) * g_ref[...]).astype(h_ref.dtype)


def _rmsnorm(x, g, out_dtype):
    t, d = x.shape
    bm = _blk(t, 512)
    return pl.pallas_call(
        _rmsnorm_kernel,
        grid=(t // bm,),
        in_specs=[pl.BlockSpec((bm, d), lambda i: (i, 0)),
                  pl.BlockSpec((1, d), lambda i: (0, 0))],
        out_specs=pl.BlockSpec((bm, d), lambda i: (i, 0)),
        out_shape=jax.ShapeDtypeStruct((t, d), out_dtype),
        compiler_params=_params("parallel"),
        name="rmsnorm",
    )(x, g.reshape(1, d))


def _add_rmsnorm_kernel(x_ref, d_ref, g_ref, xo_ref, h_ref):
    x = x_ref[...] + d_ref[...]
    xo_ref[...] = x
    ms = jnp.mean(x * x, axis=-1, keepdims=True)
    h_ref[...] = (x * lax.rsqrt(ms + EPS) * g_ref[...]).astype(h_ref.dtype)


def _add_rmsnorm(x, delta, g, out_dtype):
    t, d = x.shape
    bm = _blk(t, 256)
    return pl.pallas_call(
        _add_rmsnorm_kernel,
        grid=(t // bm,),
        in_specs=[pl.BlockSpec((bm, d), lambda i: (i, 0)),
                  pl.BlockSpec((bm, d), lambda i: (i, 0)),
                  pl.BlockSpec((1, d), lambda i: (0, 0))],
        out_specs=[pl.BlockSpec((bm, d), lambda i: (i, 0)),
                   pl.BlockSpec((bm, d), lambda i: (i, 0))],
        out_shape=[jax.ShapeDtypeStruct((t, d), _F32),
                   jax.ShapeDtypeStruct((t, d), out_dtype)],
        compiler_params=_params("parallel"),
        name="add_rmsnorm",
    )(x, delta, g.reshape(1, d))


def _mm_kernel(x_ref, w_ref, o_ref):
    o_ref[...] = jnp.dot(x_ref[...], w_ref[...],
                         preferred_element_type=_F32).astype(o_ref.dtype)


def _mm(x, w, l, out_dtype, bm=1024, bn=1024):
    m, k = x.shape
    n = w.shape[2]
    bm, bn = _blk(m, bm), _blk(n, bn)
    return pl.pallas_call(
        _mm_kernel,
        grid=(m // bm, n // bn),
        in_specs=[pl.BlockSpec((bm, k), lambda i, j: (i, 0)),
                  pl.BlockSpec((None, k, bn), lambda i, j: (l, 0, j))],
        out_specs=pl.BlockSpec((bm, bn), lambda i, j: (i, j)),
        out_shape=jax.ShapeDtypeStruct((m, n), out_dtype),
        compiler_params=_params("parallel", "parallel"),
        name="matmul",
    )(x, w)


def _mm_retrieve_kernel(x_ref, w_ref, sc_ref, o_ref, i1_ref, i2_ref, g_ref):
    cw = min(sc_ref.shape[2], RETRIEVE_CHUNK_LANES)
    parts = sc_ref.shape[2] // cw
    rows = x_ref.shape[0] // parts

    def part(c, carry):
        lanes = pl.ds(pl.multiple_of(c * cw, cw), cw)
        i1, i2, g = _retrieve_chunk(sc_ref[0, :, lanes], sc_ref[1, :, lanes])
        i1_ref[:, lanes] = i1
        i2_ref[:, lanes] = i2
        g_ref[:, lanes] = g
        r = pl.ds(pl.multiple_of(c * rows, rows), rows)
        o_ref[r, :] = jnp.dot(x_ref[r, :], w_ref[...],
                              preferred_element_type=_F32).astype(o_ref.dtype)
        return carry

    lax.fori_loop(0, parts, part, 0)


def _mm_hosting(x, w, l, out_dtype, scores, bm=512, bn=2304):
    m, k = x.shape
    n = w.shape[2]
    bm, bn = _blk(m, bm), _blk(n, bn)
    steps_m = m // bm
    steps = steps_m * (n // bn)
    hs, keys_n, tg = scores.shape
    heads = hs // 2
    topk = PEER_TOPK
    assert (tg * heads) % steps == 0
    tmg = tg * heads // steps
    assert tmg % LANES == 0 and tg % tmg == 0
    tiles = tg // tmg

    def head(j, i):
        return (j * steps_m + i) // tiles

    def tile(j, i):
        return (j * steps_m + i) % tiles

    r_shape = jax.ShapeDtypeStruct((heads * topk, tg), _F32)
    r_spec = pl.BlockSpec((topk, tmg), lambda j, i: (head(j, i), tile(j, i)))
    out, i1, i2, g = pl.pallas_call(
        _mm_retrieve_kernel,
        grid=(n // bn, steps_m),
        in_specs=[pl.BlockSpec((bm, k), lambda j, i: (i, 0)),
                  pl.BlockSpec((None, k, bn), lambda j, i: (l, 0, j)),
                  pl.BlockSpec((2, keys_n, tmg), lambda j, i: (head(j, i), 0, tile(j, i)))],
        out_specs=[pl.BlockSpec((bm, bn), lambda j, i: (i, j)), r_spec, r_spec, r_spec],
        out_shape=[jax.ShapeDtypeStruct((m, n), out_dtype), r_shape, r_shape, r_shape],
        compiler_params=_params("parallel", "parallel"),
        name="matmul_retrieve",
    )(x, w, scores)
    return out, (i1, i2, g)


def _mm_res_kernel(x_ref, w_ref, r_ref, o_ref):
    o_ref[...] = r_ref[...] + jnp.dot(x_ref[...], w_ref[...], preferred_element_type=_F32)


def _mm_res(x, w, l, res, bm=1024, bn=1024):
    m, k = x.shape
    n = w.shape[2]
    bm, bn = _blk(m, bm), _blk(n, bn)
    return pl.pallas_call(
        _mm_res_kernel,
        grid=(m // bm, n // bn),
        in_specs=[pl.BlockSpec((bm, k), lambda i, j: (i, 0)),
                  pl.BlockSpec((None, k, bn), lambda i, j: (l, 0, j)),
                  pl.BlockSpec((bm, bn), lambda i, j: (i, j))],
        out_specs=pl.BlockSpec((bm, bn), lambda i, j: (i, j)),
        out_shape=jax.ShapeDtypeStruct((m, n), _F32),
        compiler_params=_params("parallel", "parallel"),
        name="matmul_residual",
    )(x, w, res)


def _merge_kernel(ya_ref, yb_ref, pa_ref, pb_ref, ga_ref, gb_ref, o_ref):
    a = jnp.dot(ya_ref[...], pa_ref[...], preferred_element_type=_F32)
    b = jnp.dot(yb_ref[...], pb_ref[...], preferred_element_type=_F32)
    ga = jax.nn.sigmoid(ga_ref[...].astype(_F32))
    gb = jax.nn.sigmoid(gb_ref[...].astype(_F32))
    o_ref[...] = (ga * a + gb * b).astype(o_ref.dtype)


def _merge(ya, yb, pa, pb, l, z, gate_col0, bm=1024, bn=1024):
    m, k = ya.shape
    n = pa.shape[2]
    bm, bn = _blk(m, bm), _blk(math.gcd(n, gate_col0), bn)
    ga0 = gate_col0 // bn
    gb0 = (gate_col0 + n) // bn
    return pl.pallas_call(
        _merge_kernel,
        grid=(m // bm, n // bn),
        in_specs=[pl.BlockSpec((bm, k), lambda i, j: (i, 0)),
                  pl.BlockSpec((bm, k), lambda i, j: (i, 0)),
                  pl.BlockSpec((None, k, bn), lambda i, j: (l, 0, j)),
                  pl.BlockSpec((None, k, bn), lambda i, j: (l, 0, j)),
                  pl.BlockSpec((bm, bn), lambda i, j: (i, ga0 + j)),
                  pl.BlockSpec((bm, bn), lambda i, j: (i, gb0 + j))],
        out_specs=pl.BlockSpec((bm, bn), lambda i, j: (i, j)),
        out_shape=jax.ShapeDtypeStruct((m, n), _BF16),
        compiler_params=_params("parallel", "parallel"),
        name="merge_matmul",
    )(ya, yb, pa, pb, z, z)


def _mixer_kernel(xin_ref, gb_ref, gc_ref, u_ref, v_ref,
                  xin_p_ref, gc_p_ref, xin_n_ref, gc_n_ref,
                  cw_ref, lng_ref, lnb_ref, sw_ref, sb_ref,
                  ya_ref, yb_ref, *, seq_len, chunk, groups):
    i = pl.program_id(0)
    tm, width = xin_ref.shape
    zc = gc_ref[...].astype(_F32) * xin_ref[...].astype(_F32)
    zp = gc_p_ref[...].astype(_F32) * xin_p_ref[...].astype(_F32)
    zn = gc_n_ref[...].astype(_F32) * xin_n_ref[...].astype(_F32)
    row0 = i * tm
    has_prev = (row0 % seq_len != 0).astype(_F32)
    has_next = ((row0 + tm) % seq_len != 0).astype(_F32)
    halo_p = zp[BF16_SUBLANES - 1:BF16_SUBLANES, :] * has_prev
    halo_n = zn[0:1, :] * has_next
    rows = lax.broadcasted_iota(jnp.int32, (tm, width), 0)
    prev = jnp.where(rows == 0, halo_p, pltpu.roll(zc, 1, 0))
    nxt = jnp.where(rows == tm - 1, halo_n, pltpu.roll(zc, tm - 1, 0))
    cw = cw_ref[...]
    conv = prev * cw[0:1, :] + zc * cw[1:2, :] + nxt * cw[2:3, :]
    ya_ref[...] = (gb_ref[...].astype(_F32) * conv).astype(ya_ref.dtype)

    v = _gelu(v_ref[...].astype(_F32))
    mu = jnp.mean(v, axis=-1, keepdims=True)
    vc = v - mu
    var = jnp.mean(vc * vc, axis=-1, keepdims=True)
    vn = (vc * lax.rsqrt(var + EPS) * lng_ref[...] + lnb_ref[...]).astype(_BF16)
    gd = width // groups
    for c in range(tm // chunk):
        r = slice(c * chunk, (c + 1) * chunk)
        for g in range(groups):
            cols = slice(g * gd, (g + 1) * gd)
            mixed = jnp.dot(sw_ref[g], vn[r, cols], preferred_element_type=_F32)
            mixed = mixed + sb_ref[:, cols]
            u = _gelu(u_ref[r, cols].astype(_F32))
            yb_ref[r, cols] = (u * mixed).astype(yb_ref.dtype)


def _mixer(z, conv_w, ln_g, ln_b, sg_w, sg_b, seq_len):
    t = z.shape[0]
    width = conv_w.shape[1]
    groups, chunk, _ = sg_w.shape
    assert ln_g.shape[0] == width, "conv and spatial-gating widths must match"
    tm = _blk(seq_len, 2 * chunk)
    assert tm % chunk == 0 and tm % BF16_SUBLANES == 0
    hb = tm // BF16_SUBLANES
    last_hb = t // BF16_SUBLANES - 1
    gd = width // groups
    sb_full = jnp.repeat(sg_b.T, gd, axis=1)

    def col(c):
        return pl.BlockSpec((tm, width), lambda i: (i, c))

    def halo_prev(c):
        return pl.BlockSpec((BF16_SUBLANES, width),
                            lambda i: (jnp.maximum(i * hb - 1, 0), c))

    def halo_next(c):
        return pl.BlockSpec((BF16_SUBLANES, width),
                            lambda i: (jnp.minimum((i + 1) * hb, last_hb), c))

    def full(shape):
        return pl.BlockSpec(shape, lambda i: (0,) * len(shape))

    kern = functools.partial(_mixer_kernel, seq_len=seq_len, chunk=chunk, groups=groups)
    return pl.pallas_call(
        kern,
        grid=(t // tm,),
        in_specs=[col(0), col(1), col(2), col(3), col(4),
                  halo_prev(0), halo_prev(2), halo_next(0), halo_next(2),
                  full((3, width)), full((1, width)), full((1, width)),
                  full((groups, chunk, chunk)), full((chunk, width))],
        out_specs=[pl.BlockSpec((tm, width), lambda i: (i, 0)),
                   pl.BlockSpec((tm, width), lambda i: (i, 0))],
        out_shape=[jax.ShapeDtypeStruct((t, width), _BF16),
                   jax.ShapeDtypeStruct((t, width), _BF16)],
        compiler_params=_params("parallel"),
        name="mixer",
    )(z, z, z, z, z, z, z, z, z,
      conv_w, ln_g.reshape(1, width), ln_b.reshape(1, width),
      sg_w.astype(_BF16), sb_full)


def _top_rows(sc, k):
    n = sc.shape[0]
    rows = lax.broadcasted_iota(jnp.int32, sc.shape, 0)
    vals, idxs = [], []
    for _ in range(k):
        m = jnp.max(sc, axis=0, keepdims=True)
        idx = jnp.min(jnp.where(sc == m, rows, n), axis=0, keepdims=True)
        sc = jnp.where(rows == idx, -jnp.inf, sc)
        vals.append(m)
        idxs.append(idx)
    return vals, idxs


def _candidate_groups(k):
    return [(k1, k // (k1 + 1)) for k1 in range(k)]


def _query_scores_kernel(h_ref, wq_ref, keys_ref, sc_ref):
    hk = keys_ref.shape[2]
    q = jnp.dot(h_ref[...], wq_ref[...], preferred_element_type=_F32)
    for hs in range(keys_ref.shape[0]):
        sc_ref[hs] = lax.dot_general(keys_ref[hs], q[:, hs * hk:(hs + 1) * hk],
                                     (((1,), (1,)), ((), ())),
                                     precision=lax.Precision.HIGHEST,
                                     preferred_element_type=_F32)


def _query_scores(h, wq, l, keys, bm=1024, bn=1024):
    t, d = h.shape
    heads, _, n_keys, hk = keys.shape
    n = heads * 2 * hk
    bm, bn = _blk(t, bm), _blk(n, bn)
    assert bn % hk == 0
    per = bn // hk
    return pl.pallas_call(
        _query_scores_kernel,
        grid=(t // bm, n // bn),
        in_specs=[pl.BlockSpec((bm, d), lambda i, j: (i, 0)),
                  pl.BlockSpec((None, d, bn), lambda i, j: (l, 0, j)),
                  pl.BlockSpec((per, n_keys, hk), lambda i, j: (j, 0, 0))],
        out_specs=pl.BlockSpec((per, n_keys, bm), lambda i, j: (j, 0, i)),
        out_shape=jax.ShapeDtypeStruct((heads * 2, n_keys, t), _F32),
        compiler_params=_params("parallel", "parallel"),
        name="query_scores",
    )(h, wq, keys.reshape(heads * 2, n_keys, hk))


def _retrieve_kernel(sc_ref, i1_ref, i2_ref, g_ref):
    tm = sc_ref.shape[2]
    cw = min(tm, RETRIEVE_CHUNK_LANES)
    for c in range(tm // cw):
        cols = slice(c * cw, (c + 1) * cw)
        i1, i2, g = _retrieve_chunk(sc_ref[0, :, cols], sc_ref[1, :, cols])
        i1_ref[:, cols] = i1
        i2_ref[:, cols] = i2
        g_ref[:, cols] = g


def _retrieve_chunk(sc1, sc2):
    k = PEER_TOPK
    (s1, i1), (s2, i2) = _top_rows(sc1, k), _top_rows(sc2, k)
    cand, c1, c2 = [], [], []
    for k1, n2 in _candidate_groups(k):
        for k2 in range(n2):
            cand.append(s1[k1] + s2[k2])
            c1.append(i1[k1])
            c2.append(i2[k2])
    n_c = len(cand)
    pad = (-n_c) % 8
    lanes = cand[0].shape[1]
    cand = jnp.concatenate(cand + [jnp.full((pad, lanes), -jnp.inf, _F32)], axis=0)
    c1 = jnp.concatenate(c1 + [jnp.zeros((pad, lanes), jnp.int32)], axis=0)
    c2 = jnp.concatenate(c2 + [jnp.zeros((pad, lanes), jnp.int32)], axis=0)
    rows = lax.broadcasted_iota(jnp.int32, cand.shape, 0)
    top, e1, e2 = [], [], []
    for _ in range(k):
        m = jnp.max(cand, axis=0, keepdims=True)
        idx = jnp.min(jnp.where(cand == m, rows, n_c + pad), axis=0, keepdims=True)
        sel = rows == idx
        e1.append(jnp.sum(jnp.where(sel, c1, 0), axis=0, keepdims=True))
        e2.append(jnp.sum(jnp.where(sel, c2, 0), axis=0, keepdims=True))
        cand = jnp.where(sel, -jnp.inf, cand)
        top.append(m)
    top = jnp.concatenate(top, axis=0)
    ex = jnp.exp(top - top[0:1, :])
    gates = ex / jnp.sum(ex, axis=0, keepdims=True)
    return (jnp.concatenate(e1, axis=0).astype(_F32),
            jnp.concatenate(e2, axis=0).astype(_F32), gates)


def _retrieve(sc):
    hs, n_keys, t = sc.shape
    heads = hs // 2
    tm = _blk(t, 256)
    k = PEER_TOPK
    out = jax.ShapeDtypeStruct((heads * k, t), _F32)
    ospec = pl.BlockSpec((k, tm), lambda i, h: (h, i))
    return pl.pallas_call(
        _retrieve_kernel,
        grid=(t // tm, heads),
        in_specs=[pl.BlockSpec((2, n_keys, tm), lambda i, h: (h, 0, i))],
        out_specs=[ospec, ospec, ospec],
        out_shape=[out, out, out],
        compiler_params=_params("parallel", "parallel"),
        name="peer_retrieve",
    )(sc)


def _gate_matrix_kernel(i1_ref, i2_ref, g_ref, w_ref, s_ref, *, n_keys):
    tm, slots = i1_ref.shape
    i1 = i1_ref[...]
    i2 = i2_ref[...]
    gt = g_ref[...]
    key_ids = lax.broadcasted_iota(jnp.int32, (n_keys, slots), 0).astype(_F32).astype(_BF16)
    one = jnp.ones((n_keys, slots), _BF16)
    zero = jnp.zeros((n_keys, slots), _BF16)
    for t in range(tm):
        r1 = jnp.broadcast_to(i1[t:t + 1, :].astype(_BF16), (n_keys, slots))
        r2 = jnp.broadcast_to(i2[t:t + 1, :].astype(_BF16), (n_keys, slots))
        rg = jnp.broadcast_to(gt[t:t + 1, :].astype(_BF16), (n_keys, slots))
        a_hot = jnp.where(key_ids == r1, one, zero)
        b_val = jnp.where(key_ids == r2, rg, zero)
        w_t = lax.dot_general(a_hot, b_val, (((1,), (1,)), ((), ())),
                              preferred_element_type=_F32)
        s_ref[t * W_PITCH:t * W_PITCH + n_keys, :] = w_t
    for a in range(n_keys):
        w_ref[:, a * n_keys:(a + 1) * n_keys] = (
            s_ref[pl.ds(a, tm, stride=W_PITCH), :].astype(w_ref.dtype))


def _gate_matrix(i1, i2, gates, n_keys):
    t, slots = i1.shape
    assert n_keys == LANES
    tm = _blk(t, LANES)
    spec = pl.BlockSpec((tm, slots), lambda i: (i, 0))
    return pl.pallas_call(
        functools.partial(_gate_matrix_kernel, n_keys=n_keys),
        grid=(t // tm,),
        in_specs=[spec, spec, spec],
        out_specs=pl.BlockSpec((tm, n_keys * n_keys), lambda i: (i, 0)),
        out_shape=jax.ShapeDtypeStruct((t, n_keys * n_keys), _BF16),
        scratch_shapes=[pltpu.VMEM((tm * W_PITCH, LANES), _F32)],
        compiler_params=_params("parallel"),
        name="peer_gate_matrix",
    )(i1, i2, gates)


def _experts_kernel(h_ref, ut_ref, v_ref, w_ref, o_ref):
    n = pl.program_id(1)
    pre = jnp.dot(h_ref[...], ut_ref[...], preferred_element_type=_F32)
    act = (w_ref[...].astype(_F32) * _gelu(pre)).astype(_BF16)

    @pl.when(n == 0)
    def _():
        o_ref[...] = jnp.zeros_like(o_ref)

    o_ref[...] += jnp.dot(act, v_ref[...], preferred_element_type=_F32)


def _experts_retrieve_kernel(h_ref, ut_ref, v_ref, w_ref, sc_ref,
                             o_ref, i1_ref, i2_ref, g_ref):
    _retrieve_kernel(sc_ref, i1_ref, i2_ref, g_ref)
    _experts_kernel(h_ref, ut_ref, v_ref, w_ref, o_ref)


def _experts(h, ut, v, l, w, guest=None, bm=512, bn=1024):
    t, d = h.shape
    e = ut.shape[2]
    bm, bn = _blk(t, bm), _blk(e, bn)
    steps_n = e // bn
    once = pl.Buffered(1)
    in_specs = [pl.BlockSpec((bm, d), lambda i, n: (i, 0), pipeline_mode=once),
                pl.BlockSpec((None, d, bn), lambda i, n: (l, 0, n)),
                pl.BlockSpec((None, bn, d), lambda i, n: (l, n, 0)),
                pl.BlockSpec((bm, bn), lambda i, n: (i, n))]
    out_spec = pl.BlockSpec((bm, d), lambda i, n: (i, 0), pipeline_mode=once)
    out_shape = jax.ShapeDtypeStruct((t, d), _F32)
    if guest is None:
        return pl.pallas_call(
            _experts_kernel,
            grid=(t // bm, steps_n),
            in_specs=in_specs,
            out_specs=out_spec,
            out_shape=out_shape,
            compiler_params=_params("parallel", "arbitrary"),
            name="peer_experts",
        )(h, ut, v, w), None

    hs, n_keys, tg = guest.shape
    heads = hs // 2
    k = PEER_TOPK
    steps = (t // bm) * steps_n
    assert (tg * heads) % steps == 0
    tmg = tg * heads // steps
    assert tmg % LANES == 0 and tg % tmg == 0

    def tile(i, n):
        return (i * steps_n + n) // heads

    def head(i, n):
        return (i * steps_n + n) % heads

    r_shape = jax.ShapeDtypeStruct((heads * k, tg), _F32)
    r_spec = pl.BlockSpec((k, tmg), lambda i, n: (head(i, n), tile(i, n)))
    delta, i1, i2, g = pl.pallas_call(
        _experts_retrieve_kernel,
        grid=(t // bm, steps_n),
        in_specs=in_specs + [
            pl.BlockSpec((2, n_keys, tmg), lambda i, n: (head(i, n), 0, tile(i, n)))],
        out_specs=[out_spec, r_spec, r_spec, r_spec],
        out_shape=[out_shape, r_shape, r_shape, r_shape],
        compiler_params=_params("parallel", "arbitrary"),
        name="peer_experts_retrieve",
    )(h, ut, v, w, guest)
    return delta, (i1, i2, g)


def _mixer_block(x, h, p, l, seq_len, scores=None):
    width = p["conv_w"].shape[-1]
    if scores is None:
        z, guest = _mm(h, p["w_in"], l, _BF16), None
    else:
        z, guest = _mm_hosting(h, p["w_in"], l, _BF16, scores)
    ya, yb = _mixer(z, p["conv_w"][l], p["sg_ln_g"][l], p["sg_ln_b"][l],
                    p["sg_w"][l], p["sg_b"][l], seq_len)
    merged = _merge(ya, yb, p["proj_a"], p["proj_b"], l, z, 5 * width)
    x = _mm_res(merged, p["w_o"], l, x)
    h = _rmsnorm(x, p["norm2_g"][l], _BF16)
    sc = _query_scores(h, p["peer_wq"], l, p["peer_keys"][l])
    return x, h, sc, guest


def _slots_last(picks):
    return tuple(a.T for a in picks)


def _trunk_pair(xa3, xb3, p):
    depth = p["w_in"].shape[0]
    n_keys = p["peer_keys"].shape[-2]
    d = xa3.shape[-1]
    seq_a, seq_b = xa3.shape[1], xb3.shape[1]
    xa, xb = xa3.reshape(-1, d), xb3.reshape(-1, d)
    ha = _rmsnorm(xa, p["norm1_g"][0], _BF16)
    hb = _rmsnorm(xb, p["norm1_g"][0], _BF16)
    xa, ha, sa, _ = _mixer_block(xa, ha, p, 0, seq_a)
    xb, hb, sb, ra = _mixer_block(xb, hb, p, 0, seq_b, scores=sa)
    wa = _gate_matrix(*_slots_last(ra), n_keys)
    for l in range(depth):
        last = l + 1 == depth
        gain, out_dtype = (p["final_g"], _F32) if last else (p["norm1_g"][l + 1], _BF16)
        da, rb = _experts(ha, p["peer_ut"], p["peer_v"], l, wa, guest=sb)
        xa, ha = _add_rmsnorm(xa, da, gain, out_dtype)
        wb = _gate_matrix(*_slots_last(rb), n_keys)
        if last:
            db, _ = _experts(hb, p["peer_ut"], p["peer_v"], l, wb)
            xb, hb = _add_rmsnorm(xb, db, gain, out_dtype)
        else:
            xa, ha, sa, _ = _mixer_block(xa, ha, p, l + 1, seq_a)
            db, ra = _experts(hb, p["peer_ut"], p["peer_v"], l, wb, guest=sa)
            xb, hb = _add_rmsnorm(xb, db, gain, out_dtype)
            xb, hb, sb, _ = _mixer_block(xb, hb, p, l + 1, seq_b)
            wa = _gate_matrix(*_slots_last(ra), n_keys)
    return ha.reshape(xa3.shape), hb.reshape(xb3.shape)


def kernel(x_prompt, x_sample, norm1_g, w_in, conv_w, sg_ln_g, sg_ln_b, sg_w, sg_b, proj_a, proj_b, w_o, norm2_g, peer_wq, peer_keys, peer_u, peer_v, final_g):
    p = dict(
        norm1_g=norm1_g, conv_w=conv_w, sg_ln_g=sg_ln_g, sg_ln_b=sg_ln_b, sg_w=sg_w, sg_b=sg_b,
        norm2_g=norm2_g, peer_keys=peer_keys, final_g=final_g,
        w_in=w_in.astype(_BF16), proj_a=proj_a.astype(_BF16), proj_b=proj_b.astype(_BF16),
        w_o=w_o.astype(_BF16), peer_wq=peer_wq.astype(_BF16),
        peer_ut=jnp.swapaxes(peer_u, 1, 2).astype(_BF16), peer_v=peer_v.astype(_BF16),
    )
    return _trunk_pair(x_prompt, x_sample, p)
```

```python
import functools
import math

import jax
import jax.numpy as jnp
from jax import lax
from jax.experimental import pallas as pl
from jax.experimental.pallas import tpu as pltpu

PEER_TOPK = 16
EPS = 1e-6
V7X_VMEM_LIMIT_BYTES = 56 * 1024 * 1024
BF16_SUBLANES = 16
LANES = 128
W_PITCH = 132
W_PITCH_SHIFT = W_PITCH % 8
RETRIEVE_CHUNK_LANES = 256

_F32 = jnp.float32
_BF16 = jnp.bfloat16


def _params(*semantics):
    return pltpu.CompilerParams(dimension_semantics=semantics,
                                vmem_limit_bytes=V7X_VMEM_LIMIT_BYTES)


def _blk(n, target):
    if n <= target:
        return n
    for b in range(target - target % LANES, 0, -LANES):
        if n % b == 0:
            return b
    raise ValueError(f"no lane-aligned block for {n} <= {target}")


def _gelu(x):
    return 0.5 * x * (1.0 + lax.erf(x * (1.0 / math.sqrt(2.0))))


def _rmsnorm_kernel(x_ref, g_ref, h_ref):
    x = x_ref[...]
    ms = jnp.mean(x * x, axis=-1, keepdims=True)
    h_ref[...] = (x * lax.rsqrt(ms + EPS) * g_ref[...]).astype(h_ref.dtype)


def _rmsnorm(x, g, out_dtype):
    t, d = x.shape
    bm = _blk(t, 512)
    return pl.pallas_call(
        _rmsnorm_kernel,
        grid=(t // bm,),
        in_specs=[pl.BlockSpec((bm, d), lambda i: (i, 0)),
                  pl.BlockSpec((1, d), lambda i: (0, 0))],
        out_specs=pl.BlockSpec((bm, d), lambda i: (i, 0)),
        out_shape=jax.ShapeDtypeStruct((t, d), out_dtype),
        compiler_params=_params("parallel"),
        name="rmsnorm",
    )(x, g.reshape(1, d))


def _add_rmsnorm_kernel(x_ref, d_ref, g_ref, xo_ref, h_ref):
    x = x_ref[...] + d_ref[...]
    xo_ref[...] = x
    ms = jnp.mean(x * x, axis=-1, keepdims=True)
    h_ref[...] = (x * lax.rsqrt(ms + EPS) * g_ref[...]).astype(h_ref.dtype)


def _add_rmsnorm(x, delta, g, out_dtype):
    t, d = x.shape
    bm = _blk(t, 256)
    return pl.pallas_call(
        _add_rmsnorm_kernel,
        grid=(t // bm,),
        in_specs=[pl.BlockSpec((bm, d), lambda i: (i, 0)),
                  pl.BlockSpec((bm, d), lambda i: (i, 0)),
                  pl.BlockSpec((1, d), lambda i: (0, 0))],
        out_specs=[pl.BlockSpec((bm, d), lambda i: (i, 0)),
                   pl.BlockSpec((bm, d), lambda i: (i, 0))],
        out_shape=[jax.ShapeDtypeStruct((t, d), _F32),
                   jax.ShapeDtypeStruct((t, d), out_dtype)],
        compiler_params=_params("parallel"),
        name="add_rmsnorm",
    )(x, delta, g.reshape(1, d))


def _mm_kernel(x_ref, w_ref, o_ref):
    o_ref[...] = jnp.dot(x_ref[...], w_ref[...],
                         preferred_element_type=_F32).astype(o_ref.dtype)


def _mm(x, w, l, out_dtype, bm=1024, bn=1024):
    m, k = x.shape
    n = w.shape[2]
    bm, bn = _blk(m, bm), _blk(n, bn)
    return pl.pallas_call(
        _mm_kernel,
        grid=(m // bm, n // bn),
        in_specs=[pl.BlockSpec((bm, k), lambda i, j: (i, 0)),
                  pl.BlockSpec((None, k, bn), lambda i, j: (l, 0, j))],
        out_specs=pl.BlockSpec((bm, bn), lambda i, j: (i, j)),
        out_shape=jax.ShapeDtypeStruct((m, n), out_dtype),
        compiler_params=_params("parallel", "parallel"),
        name="matmul",
    )(x, w)


def _mm_retrieve_kernel(x_ref, w_ref, sc_ref, o_ref, i1_ref, i2_ref, g_ref):
    cw = min(sc_ref.shape[2], RETRIEVE_CHUNK_LANES)
    parts = sc_ref.shape[2] // cw
    rows = x_ref.shape[0] // parts

    def part(c, carry):
        lanes = pl.ds(pl.multiple_of(c * cw, cw), cw)
        i1, i2, g = _retrieve_chunk(sc_ref[0, :, lanes], sc_ref[1, :, lanes])
        i1_ref[:, lanes] = i1
        i2_ref[:, lanes] = i2
        g_ref[:, lanes] = g
        r = pl.ds(pl.multiple_of(c * rows, rows), rows)
        o_ref[r, :] = jnp.dot(x_ref[r, :], w_ref[...],
                              preferred_element_type=_F32).astype(o_ref.dtype)
        return carry

    lax.fori_loop(0, parts, part, 0)


def _mm_hosting(x, w, l, out_dtype, scores, bm=512, bn=2304):
    m, k = x.shape
    n = w.shape[2]
    bm, bn = _blk(m, bm), _blk(n, bn)
    steps_m = m // bm
    steps = steps_m * (n // bn)
    hs, keys_n, tg = scores.shape
    heads = hs // 2
    topk = PEER_TOPK
    assert (tg * heads) % steps == 0
    tmg = tg * heads // steps
    assert tmg % LANES == 0 and tg % tmg == 0
    tiles = tg // tmg

    def head(j, i):
        return (j * steps_m + i) // tiles

    def tile(j, i):
        return (j * steps_m + i) % tiles

    r_shape = jax.ShapeDtypeStruct((heads * topk, tg), _F32)
    r_spec = pl.BlockSpec((topk, tmg), lambda j, i: (head(j, i), tile(j, i)))
    out, i1, i2, g = pl.pallas_call(
        _mm_retrieve_kernel,
        grid=(n // bn, steps_m),
        in_specs=[pl.BlockSpec((bm, k), lambda j, i: (i, 0)),
                  pl.BlockSpec((None, k, bn), lambda j, i: (l, 0, j)),
                  pl.BlockSpec((2, keys_n, tmg), lambda j, i: (head(j, i), 0, tile(j, i)))],
        out_specs=[pl.BlockSpec((bm, bn), lambda j, i: (i, j)), r_spec, r_spec, r_spec],
        out_shape=[jax.ShapeDtypeStruct((m, n), out_dtype), r_shape, r_shape, r_shape],
        compiler_params=_params("parallel", "parallel"),
        name="matmul_retrieve",
    )(x, w, scores)
    return out, (i1, i2, g)


def _mm_res_kernel(x_ref, w_ref, r_ref, o_ref):
    o_ref[...] = r_ref[...] + jnp.dot(x_ref[...], w_ref[...], preferred_element_type=_F32)


def _mm_res(x, w, l, res, bm=1024, bn=1024):
    m, k = x.shape
    n = w.shape[2]
    bm, bn = _blk(m, bm), _blk(n, bn)
    return pl.pallas_call(
        _mm_res_kernel,
        grid=(m // bm, n // bn),
        in_specs=[pl.BlockSpec((bm, k), lambda i, j: (i, 0)),
                  pl.BlockSpec((None, k, bn), lambda i, j: (l, 0, j)),
                  pl.BlockSpec((bm, bn), lambda i, j: (i, j))],
        out_specs=pl.BlockSpec((bm, bn), lambda i, j: (i, j)),
        out_shape=jax.ShapeDtypeStruct((m, n), _F32),
        compiler_params=_params("parallel", "parallel"),
        name="matmul_residual",
    )(x, w, res)


def _merge_kernel(ya_ref, yb_ref, pa_ref, pb_ref, ga_ref, gb_ref, o_ref):
    a = jnp.dot(ya_ref[...], pa_ref[...], preferred_element_type=_F32)
    b = jnp.dot(yb_ref[...], pb_ref[...], preferred_element_type=_F32)
    ga = jax.nn.sigmoid(ga_ref[...].astype(_F32))
    gb = jax.nn.sigmoid(gb_ref[...].astype(_F32))
    o_ref[...] = (ga * a + gb * b).astype(o_ref.dtype)


def _merge(ya, yb, pa, pb, l, z, gate_col0, bm=1024, bn=1024):
    m, k = ya.shape
    n = pa.shape[2]
    bm, bn = _blk(m, bm), _blk(math.gcd(n, gate_col0), bn)
    ga0 = gate_col0 // bn
    gb0 = (gate_col0 + n) // bn
    return pl.pallas_call(
        _merge_kernel,
        grid=(m // bm, n // bn),
        in_specs=[pl.BlockSpec((bm, k), lambda i, j: (i, 0)),
                  pl.BlockSpec((bm, k), lambda i, j: (i, 0)),
                  pl.BlockSpec((None, k, bn), lambda i, j: (l, 0, j)),
                  pl.BlockSpec((None, k, bn), lambda i, j: (l, 0, j)),
                  pl.BlockSpec((bm, bn), lambda i, j: (i, ga0 + j)),
                  pl.BlockSpec((bm, bn), lambda i, j: (i, gb0 + j))],
        out_specs=pl.BlockSpec((bm, bn), lambda i, j: (i, j)),
        out_shape=jax.ShapeDtypeStruct((m, n), _BF16),
        compiler_params=_params("parallel", "parallel"),
        name="merge_matmul",
    )(ya, yb, pa, pb, z, z)


def _mixer_kernel(xin_ref, gb_ref, gc_ref, u_ref, v_ref,
                  xin_p_ref, gc_p_ref, xin_n_ref, gc_n_ref,
                  cw_ref, lng_ref, lnb_ref, sw_ref, sb_ref,
                  ya_ref, yb_ref, *, seq_len, chunk, groups):
    i = pl.program_id(0)
    tm, width = xin_ref.shape
    zc = gc_ref[...].astype(_F32) * xin_ref[...].astype(_F32)
    zp = gc_p_ref[...].astype(_F32) * xin_p_ref[...].astype(_F32)
    zn = gc_n_ref[...].astype(_F32) * xin_n_ref[...].astype(_F32)
    row0 = i * tm
    has_prev = (row0 % seq_len != 0).astype(_F32)
    has_next = ((row0 + tm) % seq_len != 0).astype(_F32)
    halo_p = zp[BF16_SUBLANES - 1:BF16_SUBLANES, :] * has_prev
    halo_n = zn[0:1, :] * has_next
    rows = lax.broadcasted_iota(jnp.int32, (tm, width), 0)
    prev = jnp.where(rows == 0, halo_p, pltpu.roll(zc, 1, 0))
    nxt = jnp.where(rows == tm - 1, halo_n, pltpu.roll(zc, tm - 1, 0))
    cw = cw_ref[...]
    conv = prev * cw[0:1, :] + zc * cw[1:2, :] + nxt * cw[2:3, :]
    ya_ref[...] = (gb_ref[...].astype(_F32) * conv).astype(ya_ref.dtype)

    v = _gelu(v_ref[...].astype(_F32))
    mu = jnp.mean(v, axis=-1, keepdims=True)
    vc = v - mu
    var = jnp.mean(vc * vc, axis=-1, keepdims=True)
    vn = (vc * lax.rsqrt(var + EPS) * lng_ref[...] + lnb_ref[...]).astype(_BF16)
    gd = width // groups
    for c in range(tm // chunk):
        r = slice(c * chunk, (c + 1) * chunk)
        for g in range(groups):
            cols = slice(g * gd, (g + 1) * gd)
            mixed = jnp.dot(sw_ref[g], vn[r, cols], preferred_element_type=_F32)
            mixed = mixed + sb_ref[:, cols]
            u = _gelu(u_ref[r, cols].astype(_F32))
            yb_ref[r, cols] = (u * mixed).astype(yb_ref.dtype)


def _mixer(z, conv_w, ln_g, ln_b, sg_w, sg_b, seq_len):
    t = z.shape[0]
    width = conv_w.shape[1]
    groups, chunk, _ = sg_w.shape
    assert ln_g.shape[0] == width, "conv and spatial-gating widths must match"
    tm = _blk(seq_len, 2 * chunk)
    assert tm % chunk == 0 and tm % BF16_SUBLANES == 0
    hb = tm // BF16_SUBLANES
    last_hb = t // BF16_SUBLANES - 1
    gd = width // groups
    sb_full = jnp.repeat(sg_b.T, gd, axis=1)

    def col(c):
        return pl.BlockSpec((tm, width), lambda i: (i, c))

    def halo_prev(c):
        return pl.BlockSpec((BF16_SUBLANES, width),
                            lambda i: (jnp.maximum(i * hb - 1, 0), c))

    def halo_next(c):
        return pl.BlockSpec((BF16_SUBLANES, width),
                            lambda i: (jnp.minimum((i + 1) * hb, last_hb), c))

    def full(shape):
        return pl.BlockSpec(shape, lambda i: (0,) * len(shape))

    kern = functools.partial(_mixer_kernel, seq_len=seq_len, chunk=chunk, groups=groups)
    return pl.pallas_call(
        kern,
        grid=(t // tm,),
        in_specs=[col(0), col(1), col(2), col(3), col(4),
                  halo_prev(0), halo_prev(2), halo_next(0), halo_next(2),
                  full((3, width)), full((1, width)), full((1, width)),
                  full((groups, chunk, chunk)), full((chunk, width))],
        out_specs=[pl.BlockSpec((tm, width), lambda i: (i, 0)),
                   pl.BlockSpec((tm, width), lambda i: (i, 0))],
        out_shape=[jax.ShapeDtypeStruct((t, width), _BF16),
                   jax.ShapeDtypeStruct((t, width), _BF16)],
        compiler_params=_params("parallel"),
        name="mixer",
    )(z, z, z, z, z, z, z, z, z,
      conv_w, ln_g.reshape(1, width), ln_b.reshape(1, width),
      sg_w.astype(_BF16), sb_full)


def _top_rows(sc, k):
    n = sc.shape[0]
    rows = lax.broadcasted_iota(jnp.int32, sc.shape, 0)
    vals, idxs = [], []
    for _ in range(k):
        m = jnp.max(sc, axis=0, keepdims=True)
        idx = jnp.min(jnp.where(sc == m, rows, n), axis=0, keepdims=True)
        sc = jnp.where(rows == idx, -jnp.inf, sc)
        vals.append(m)
        idxs.append(idx)
    return vals, idxs


def _candidate_groups(k):
    return [(k1, k // (k1 + 1)) for k1 in range(k)]


def _query_scores_kernel(h_ref, wq_ref, keys_ref, sc_ref):
    hk = keys_ref.shape[2]
    q = jnp.dot(h_ref[...], wq_ref[...], preferred_element_type=_F32)
    for hs in range(keys_ref.shape[0]):
        sc_ref[hs] = lax.dot_general(keys_ref[hs], q[:, hs * hk:(hs + 1) * hk],
                                     (((1,), (1,)), ((), ())),
                                     precision=lax.Precision.HIGHEST,
                                     preferred_element_type=_F32)


def _query_scores(h, wq, l, keys, bm=1024, bn=1024):
    t, d = h.shape
    heads, _, n_keys, hk = keys.shape
    n = heads * 2 * hk
    bm, bn = _blk(t, bm), _blk(n, bn)
    assert bn % hk == 0
    per = bn // hk
    return pl.pallas_call(
        _query_scores_kernel,
        grid=(t // bm, n // bn),
        in_specs=[pl.BlockSpec((bm, d), lambda i, j: (i, 0)),
                  pl.BlockSpec((None, d, bn), lambda i, j: (l, 0, j)),
                  pl.BlockSpec((per, n_keys, hk), lambda i, j: (j, 0, 0))],
        out_specs=pl.BlockSpec((per, n_keys, bm), lambda i, j: (j, 0, i)),
        out_shape=jax.ShapeDtypeStruct((heads * 2, n_keys, t), _F32),
        compiler_params=_params("parallel", "parallel"),
        name="query_scores",
    )(h, wq, keys.reshape(heads * 2, n_keys, hk))


def _retrieve_kernel(sc_ref, i1_ref, i2_ref, g_ref):
    tm = sc_ref.shape[2]
    cw = min(tm, RETRIEVE_CHUNK_LANES)
    for c in range(tm // cw):
        cols = slice(c * cw, (c + 1) * cw)
        i1, i2, g = _retrieve_chunk(sc_ref[0, :, cols], sc_ref[1, :, cols])
        i1_ref[:, cols] = i1
        i2_ref[:, cols] = i2
        g_ref[:, cols] = g


def _retrieve_chunk(sc1, sc2):
    k = PEER_TOPK
    (s1, i1), (s2, i2) = _top_rows(sc1, k), _top_rows(sc2, k)
    cand, c1, c2 = [], [], []
    for k1, n2 in _candidate_groups(k):
        for k2 in range(n2):
            cand.append(s1[k1] + s2[k2])
            c1.append(i1[k1])
            c2.append(i2[k2])
    n_c = len(cand)
    pad = (-n_c) % 8
    lanes = cand[0].shape[1]
    cand = jnp.concatenate(cand + [jnp.full((pad, lanes), -jnp.inf, _F32)], axis=0)
    c1 = jnp.concatenate(c1 + [jnp.zeros((pad, lanes), jnp.int32)], axis=0)
    c2 = jnp.concatenate(c2 + [jnp.zeros((pad, lanes), jnp.int32)], axis=0)
    rows = lax.broadcasted_iota(jnp.int32, cand.shape, 0)
    top, e1, e2 = [], [], []
    for _ in range(k):
        m = jnp.max(cand, axis=0, keepdims=True)
        idx = jnp.min(jnp.where(cand == m, rows, n_c + pad), axis=0, keepdims=True)
        sel = rows == idx
        e1.append(jnp.sum(jnp.where(sel, c1, 0), axis=0, keepdims=True))
        e2.append(jnp.sum(jnp.where(sel, c2, 0), axis=0, keepdims=True))
        cand = jnp.where(sel, -jnp.inf, cand)
        top.append(m)
    top = jnp.concatenate(top, axis=0)
    ex = jnp.exp(top - top[0:1, :])
    gates = ex / jnp.sum(ex, axis=0, keepdims=True)
    return (jnp.concatenate(e1, axis=0).astype(_F32),
            jnp.concatenate(e2, axis=0).astype(_F32), gates)


def _retrieve(sc):
    hs, n_keys, t = sc.shape
    heads = hs // 2
    tm = _blk(t, 256)
    k = PEER_TOPK
    out = jax.ShapeDtypeStruct((heads * k, t), _F32)
    ospec = pl.BlockSpec((k, tm), lambda i, h: (h, i))
    return pl.pallas_call(
        _retrieve_kernel,
        grid=(t // tm, heads),
        in_specs=[pl.BlockSpec((2, n_keys, tm), lambda i, h: (h, 0, i))],
        out_specs=[ospec, ospec, ospec],
        out_shape=[out, out, out],
        compiler_params=_params("parallel", "parallel"),
        name="peer_retrieve",
    )(sc)


def _gate_matrix_kernel(i1_ref, i2_ref, g_ref, w_ref, s_ref, *, n_keys):
    tm, slots = i1_ref.shape
    i1 = i1_ref[...]
    i2 = i2_ref[...]
    gt = g_ref[...]
    key_ids = lax.broadcasted_iota(jnp.int32, (n_keys, slots), 0).astype(_F32).astype(_BF16)
    one = jnp.ones((n_keys, slots), _BF16)
    zero = jnp.zeros((n_keys, slots), _BF16)
    grp = BF16_SUBLANES
    n_grp = tm // grp

    ext = n_keys + 2 * BF16_SUBLANES
    ext_ids = (lax.broadcasted_iota(jnp.int32, (ext, slots), 0)
               - W_PITCH_SHIFT).astype(_F32).astype(_BF16)
    ext_one = jnp.ones((ext, slots), _BF16)
    ext_zero = jnp.zeros((ext, slots), _BF16)
    dims = (((1,), (1,)), ((), ()))

    def per_token_products(g):
        for tl in range(grp):
            t = g * grp + tl
            r2 = jnp.broadcast_to(i2[t:t + 1, :].astype(_BF16), (n_keys, slots))
            rg = jnp.broadcast_to(gt[t:t + 1, :].astype(_BF16), (n_keys, slots))
            b_val = jnp.where(key_ids == r2, rg, zero)
            row0 = tl * W_PITCH
            if tl % 2 == 0:
                r1 = jnp.broadcast_to(i1[t:t + 1, :].astype(_BF16), (n_keys, slots))
                a_hot = jnp.where(key_ids == r1, one, zero)
                s_ref[g % 2, row0:row0 + n_keys, :] = lax.dot_general(
                    a_hot, b_val, dims, preferred_element_type=_F32)
            else:
                r1 = jnp.broadcast_to(i1[t:t + 1, :].astype(_BF16), (ext, slots))
                a_hot = jnp.where(ext_ids == r1, ext_one, ext_zero)
                prod = lax.dot_general(a_hot, b_val, dims, preferred_element_type=_F32)
                lo = row0 - W_PITCH_SHIFT
                s_ref[g % 2, lo:lo + n_keys + 8, :] = prod[:n_keys + 8, :]

    def gather_rows(g):
        for a in range(n_keys):
            w_ref[g * grp:(g + 1) * grp, a * n_keys:(a + 1) * n_keys] = (
                s_ref[g % 2, pl.ds(a, grp, stride=W_PITCH), :].astype(w_ref.dtype))

    for g in range(n_grp + 1):
        if g < n_grp:
            per_token_products(g)
        if g > 0:
            gather_rows(g - 1)


def _gate_matrix(i1, i2, gates, n_keys):
    t, slots = i1.shape
    assert n_keys == LANES
    tm = _blk(t, LANES)
    spec = pl.BlockSpec((tm, slots), lambda i: (i, 0))
    return pl.pallas_call(
        functools.partial(_gate_matrix_kernel, n_keys=n_keys),
        grid=(t // tm,),
        in_specs=[spec, spec, spec],
        out_specs=pl.BlockSpec((tm, n_keys * n_keys), lambda i: (i, 0)),
        out_shape=jax.ShapeDtypeStruct((t, n_keys * n_keys), _BF16),
        scratch_shapes=[pltpu.VMEM((2, BF16_SUBLANES * W_PITCH, LANES), _F32)],
        compiler_params=_params("parallel"),
        name="peer_gate_matrix",
    )(i1, i2, gates)


def _experts_kernel(h_ref, ut_ref, v_ref, w_ref, o_ref):
    n = pl.program_id(1)
    pre = jnp.dot(h_ref[...], ut_ref[...], preferred_element_type=_F32)
    act = (w_ref[...].astype(_F32) * _gelu(pre)).astype(_BF16)

    @pl.when(n == 0)
    def _():
        o_ref[...] = jnp.zeros_like(o_ref)

    o_ref[...] += jnp.dot(act, v_ref[...], preferred_element_type=_F32)


def _experts_retrieve_kernel(h_ref, ut_ref, v_ref, w_ref, sc_ref,
                             o_ref, i1_ref, i2_ref, g_ref):
    _retrieve_kernel(sc_ref, i1_ref, i2_ref, g_ref)
    _experts_kernel(h_ref, ut_ref, v_ref, w_ref, o_ref)


def _experts(h, ut, v, l, w, guest=None, bm=1024, bn=512):
    t, d = h.shape
    e = ut.shape[2]
    bm, bn = _blk(t, bm), _blk(e, bn)
    steps_n = e // bn
    once = pl.Buffered(1)
    in_specs = [pl.BlockSpec((bm, d), lambda i, n: (i, 0)),
                pl.BlockSpec((None, d, bn), lambda i, n: (l, 0, n)),
                pl.BlockSpec((None, bn, d), lambda i, n: (l, n, 0)),
                pl.BlockSpec((bm, bn), lambda i, n: (i, n))]
    out_spec = pl.BlockSpec((bm, d), lambda i, n: (i, 0), pipeline_mode=once)
    out_shape = jax.ShapeDtypeStruct((t, d), _F32)
    if guest is None:
        return pl.pallas_call(
            _experts_kernel,
            grid=(t // bm, steps_n),
            in_specs=in_specs,
            out_specs=out_spec,
            out_shape=out_shape,
            compiler_params=_params("parallel", "arbitrary"),
            name="peer_experts",
        )(h, ut, v, w), None

    hs, n_keys, tg = guest.shape
    heads = hs // 2
    k = PEER_TOPK
    steps = (t // bm) * steps_n
    assert (tg * heads) % steps == 0
    tmg = tg * heads // steps
    assert tmg % LANES == 0 and tg % tmg == 0

    def tile(i, n):
        return (i * steps_n + n) // heads

    def head(i, n):
        return (i * steps_n + n) % heads

    r_shape = jax.ShapeDtypeStruct((heads * k, tg), _F32)
    r_spec = pl.BlockSpec((k, tmg), lambda i, n: (head(i, n), tile(i, n)))
    delta, i1, i2, g = pl.pallas_call(
        _experts_retrieve_kernel,
        grid=(t // bm, steps_n),
        in_specs=in_specs + [
            pl.BlockSpec((2, n_keys, tmg), lambda i, n: (head(i, n), 0, tile(i, n)))],
        out_specs=[out_spec, r_spec, r_spec, r_spec],
        out_shape=[out_shape, r_shape, r_shape, r_shape],
        compiler_params=_params("parallel", "arbitrary"),
        name="peer_experts_retrieve",
    )(h, ut, v, w, guest)
    return delta, (i1, i2, g)


def _mixer_block(x, h, p, l, seq_len, scores=None):
    width = p["conv_w"].shape[-1]
    if scores is None:
        z, guest = _mm(h, p["w_in"], l, _BF16), None
    else:
        z, guest = _mm_hosting(h, p["w_in"], l, _BF16, scores)
    ya, yb = _mixer(z, p["conv_w"][l], p["sg_ln_g"][l], p["sg_ln_b"][l],
                    p["sg_w"][l], p["sg_b"][l], seq_len)
    merged = _merge(ya, yb, p["proj_a"], p["proj_b"], l, z, 5 * width)
    x = _mm_res(merged, p["w_o"], l, x)
    h = _rmsnorm(x, p["norm2_g"][l], _BF16)
    sc = _query_scores(h, p["peer_wq"], l, p["peer_keys"][l])
    return x, h, sc, guest


def _slots_last(picks):
    return tuple(a.T for a in picks)


def _trunk_pair(xa3, xb3, p):
    depth = p["w_in"].shape[0]
    n_keys = p["peer_keys"].shape[-2]
    d = xa3.shape[-1]
    seq_a, seq_b = xa3.shape[1], xb3.shape[1]
    xa, xb = xa3.reshape(-1, d), xb3.reshape(-1, d)
    ha = _rmsnorm(xa, p["norm1_g"][0], _BF16)
    hb = _rmsnorm(xb, p["norm1_g"][0], _BF16)
    xa, ha, sa, _ = _mixer_block(xa, ha, p, 0, seq_a)
    xb, hb, sb, ra = _mixer_block(xb, hb, p, 0, seq_b, scores=sa)
    wa = _gate_matrix(*_slots_last(ra), n_keys)
    for l in range(depth):
        last = l + 1 == depth
        gain, out_dtype = (p["final_g"], _F32) if last else (p["norm1_g"][l + 1], _BF16)
        da, rb = _experts(ha, p["peer_ut"], p["peer_v"], l, wa, guest=sb)
        xa, ha = _add_rmsnorm(xa, da, gain, out_dtype)
        wb = _gate_matrix(*_slots_last(rb), n_keys)
        if last:
            db, _ = _experts(hb, p["peer_ut"], p["peer_v"], l, wb)
            xb, hb = _add_rmsnorm(xb, db, gain, out_dtype)
        else:
            xa, ha, sa, _ = _mixer_block(xa, ha, p, l + 1, seq_a)
            db, ra = _experts(hb, p["peer_ut"], p["peer_v"], l, wb, guest=sa)
            xb, hb = _add_rmsnorm(xb, db, gain, out_dtype)
            xb, hb, sb, _ = _mixer_block(xb, hb, p, l + 1, seq_b)
            wa = _gate_matrix(*_slots_last(ra), n_keys)
    return ha.reshape(xa3.shape), hb.reshape(xb3.shape)


def kernel(x_prompt, x_sample, norm1_g, w_in, conv_w, sg_ln_g, sg_ln_b, sg_w, sg_b, proj_a, proj_b, w_o, norm2_g, peer_wq, peer_keys, peer_u, peer_v, final_g):
    p = dict(
        norm1_g=norm1_g, conv_w=conv_w, sg_ln_g=sg_ln_g, sg_ln_b=sg_ln_b, sg_w=sg_w, sg_b=sg_b,
        norm2_g=norm2_g, peer_keys=peer_keys, final_g=final_g,
        w_in=w_in.astype(_BF16), proj_a=proj_a.astype(_BF16), proj_b=proj_b.astype(_BF16),
        w_o=w_o.astype(_BF16), peer_wq=peer_wq.astype(_BF16),
        peer_ut=jnp.swapaxes(peer_u, 1, 2).astype(_BF16), peer_v=peer_v.astype(_BF16),
    )
    return _trunk_pair(x_prompt, x_sample, p)
```

```python
import functools
import math

import jax
import jax.numpy as jnp
from jax import lax
from jax.experimental import pallas as pl
from jax.experimental.pallas import tpu as pltpu

PEER_TOPK = 16
EPS = 1e-6
V7X_VMEM_LIMIT_BYTES = 56 * 1024 * 1024
BF16_SUBLANES = 16
LANES = 128
W_PITCH = 132
W_PITCH_SHIFT = W_PITCH % 8
RETRIEVE_CHUNK_LANES = 256
NORM_CHUNK_ROWS = 64

_F32 = jnp.float32
_BF16 = jnp.bfloat16


def _params(*semantics):
    return pltpu.CompilerParams(dimension_semantics=semantics,
                                vmem_limit_bytes=V7X_VMEM_LIMIT_BYTES)


def _blk(n, target):
    if n <= target:
        return n
    for b in range(target - target % LANES, 0, -LANES):
        if n % b == 0:
            return b
    raise ValueError(f"no lane-aligned block for {n} <= {target}")


def _gelu(x):
    return 0.5 * x * (1.0 + lax.erf(x * (1.0 / math.sqrt(2.0))))


def _rmsnorm_kernel(x_ref, g_ref, h_ref):
    x = x_ref[...]
    ms = jnp.mean(x * x, axis=-1, keepdims=True)
    h_ref[...] = (x * lax.rsqrt(ms + EPS) * g_ref[...]).astype(h_ref.dtype)


def _rmsnorm(x, g, out_dtype):
    t, d = x.shape
    bm = _blk(t, 512)
    return pl.pallas_call(
        _rmsnorm_kernel,
        grid=(t // bm,),
        in_specs=[pl.BlockSpec((bm, d), lambda i: (i, 0)),
                  pl.BlockSpec((1, d), lambda i: (0, 0))],
        out_specs=pl.BlockSpec((bm, d), lambda i: (i, 0)),
        out_shape=jax.ShapeDtypeStruct((t, d), out_dtype),
        compiler_params=_params("parallel"),
        name="rmsnorm",
    )(x, g.reshape(1, d))


def _mm_kernel(x_ref, w_ref, o_ref):
    o_ref[...] = jnp.dot(x_ref[...], w_ref[...],
                         preferred_element_type=_F32).astype(o_ref.dtype)


def _mm(x, w, l, out_dtype, bm=1024, bn=1024):
    m, k = x.shape
    n = w.shape[2]
    bm, bn = _blk(m, bm), _blk(n, bn)
    return pl.pallas_call(
        _mm_kernel,
        grid=(m // bm, n // bn),
        in_specs=[pl.BlockSpec((bm, k), lambda i, j: (i, 0)),
                  pl.BlockSpec((None, k, bn), lambda i, j: (l, 0, j))],
        out_specs=pl.BlockSpec((bm, bn), lambda i, j: (i, j)),
        out_shape=jax.ShapeDtypeStruct((m, n), out_dtype),
        compiler_params=_params("parallel", "parallel"),
        name="matmul",
    )(x, w)


def _mm_retrieve_kernel(x_ref, w_ref, sc_ref, o_ref, i1_ref, i2_ref, g_ref):
    cw = min(sc_ref.shape[2], RETRIEVE_CHUNK_LANES)
    parts = sc_ref.shape[2] // cw
    rows = x_ref.shape[0] // parts

    def part(c, carry):
        lanes = pl.ds(pl.multiple_of(c * cw, cw), cw)
        i1, i2, g = _retrieve_chunk(sc_ref[0, :, lanes], sc_ref[1, :, lanes])
        i1_ref[:, lanes] = i1
        i2_ref[:, lanes] = i2
        g_ref[:, lanes] = g
        r = pl.ds(pl.multiple_of(c * rows, rows), rows)
        o_ref[r, :] = jnp.dot(x_ref[r, :], w_ref[...],
                              preferred_element_type=_F32).astype(o_ref.dtype)
        return carry

    lax.fori_loop(0, parts, part, 0)


def _mm_hosting(x, w, l, out_dtype, scores, bm=512, bn=2304):
    m, k = x.shape
    n = w.shape[2]
    bm, bn = _blk(m, bm), _blk(n, bn)
    steps_m = m // bm
    steps = steps_m * (n // bn)
    hs, keys_n, tg = scores.shape
    heads = hs // 2
    topk = PEER_TOPK
    assert (tg * heads) % steps == 0
    tmg = tg * heads // steps
    assert tmg % LANES == 0 and tg % tmg == 0
    tiles = tg // tmg

    def head(j, i):
        return (j * steps_m + i) // tiles

    def tile(j, i):
        return (j * steps_m + i) % tiles

    r_shape = jax.ShapeDtypeStruct((heads * topk, tg), _F32)
    r_spec = pl.BlockSpec((topk, tmg), lambda j, i: (head(j, i), tile(j, i)))
    out, i1, i2, g = pl.pallas_call(
        _mm_retrieve_kernel,
        grid=(n // bn, steps_m),
        in_specs=[pl.BlockSpec((bm, k), lambda j, i: (i, 0)),
                  pl.BlockSpec((None, k, bn), lambda j, i: (l, 0, j)),
                  pl.BlockSpec((2, keys_n, tmg), lambda j, i: (head(j, i), 0, tile(j, i)))],
        out_specs=[pl.BlockSpec((bm, bn), lambda j, i: (i, j)), r_spec, r_spec, r_spec],
        out_shape=[jax.ShapeDtypeStruct((m, n), out_dtype), r_shape, r_shape, r_shape],
        compiler_params=_params("parallel", "parallel"),
        name="matmul_retrieve",
    )(x, w, scores)
    return out, (i1, i2, g)


def _mm_res_kernel(x_ref, w_ref, r_ref, o_ref):
    o_ref[...] = r_ref[...] + jnp.dot(x_ref[...], w_ref[...], preferred_element_type=_F32)


def _mm_res(x, w, l, res, bm=1024, bn=1024):
    m, k = x.shape
    n = w.shape[2]
    bm, bn = _blk(m, bm), _blk(n, bn)
    return pl.pallas_call(
        _mm_res_kernel,
        grid=(m // bm, n // bn),
        in_specs=[pl.BlockSpec((bm, k), lambda i, j: (i, 0)),
                  pl.BlockSpec((None, k, bn), lambda i, j: (l, 0, j)),
                  pl.BlockSpec((bm, bn), lambda i, j: (i, j))],
        out_specs=pl.BlockSpec((bm, bn), lambda i, j: (i, j)),
        out_shape=jax.ShapeDtypeStruct((m, n), _F32),
        compiler_params=_params("parallel", "parallel"),
        name="matmul_residual",
    )(x, w, res)


def _merge_kernel(ya_ref, yb_ref, pa_ref, pb_ref, ga_ref, gb_ref, o_ref):
    a = jnp.dot(ya_ref[...], pa_ref[...], preferred_element_type=_F32)
    b = jnp.dot(yb_ref[...], pb_ref[...], preferred_element_type=_F32)
    ga = jax.nn.sigmoid(ga_ref[...].astype(_F32))
    gb = jax.nn.sigmoid(gb_ref[...].astype(_F32))
    o_ref[...] = (ga * a + gb * b).astype(o_ref.dtype)


def _merge(ya, yb, pa, pb, l, z, gate_col0, bm=1024, bn=1024):
    m, k = ya.shape
    n = pa.shape[2]
    bm, bn = _blk(m, bm), _blk(math.gcd(n, gate_col0), bn)
    ga0 = gate_col0 // bn
    gb0 = (gate_col0 + n) // bn
    return pl.pallas_call(
        _merge_kernel,
        grid=(m // bm, n // bn),
        in_specs=[pl.BlockSpec((bm, k), lambda i, j: (i, 0)),
                  pl.BlockSpec((bm, k), lambda i, j: (i, 0)),
                  pl.BlockSpec((None, k, bn), lambda i, j: (l, 0, j)),
                  pl.BlockSpec((None, k, bn), lambda i, j: (l, 0, j)),
                  pl.BlockSpec((bm, bn), lambda i, j: (i, ga0 + j)),
                  pl.BlockSpec((bm, bn), lambda i, j: (i, gb0 + j))],
        out_specs=pl.BlockSpec((bm, bn), lambda i, j: (i, j)),
        out_shape=jax.ShapeDtypeStruct((m, n), _BF16),
        compiler_params=_params("parallel", "parallel"),
        name="merge_matmul",
    )(ya, yb, pa, pb, z, z)


def _mixer_kernel(xin_ref, gb_ref, gc_ref, u_ref, v_ref,
                  xin_p_ref, gc_p_ref, xin_n_ref, gc_n_ref,
                  cw_ref, lng_ref, lnb_ref, sw_ref, sb_ref,
                  ya_ref, yb_ref, *, seq_len, chunk, groups):
    i = pl.program_id(0)
    tm, width = xin_ref.shape
    zc = gc_ref[...].astype(_F32) * xin_ref[...].astype(_F32)
    zp = gc_p_ref[...].astype(_F32) * xin_p_ref[...].astype(_F32)
    zn = gc_n_ref[...].astype(_F32) * xin_n_ref[...].astype(_F32)
    row0 = i * tm
    has_prev = (row0 % seq_len != 0).astype(_F32)
    has_next = ((row0 + tm) % seq_len != 0).astype(_F32)
    halo_p = zp[BF16_SUBLANES - 1:BF16_SUBLANES, :] * has_prev
    halo_n = zn[0:1, :] * has_next
    rows = lax.broadcasted_iota(jnp.int32, (tm, width), 0)
    prev = jnp.where(rows == 0, halo_p, pltpu.roll(zc, 1, 0))
    nxt = jnp.where(rows == tm - 1, halo_n, pltpu.roll(zc, tm - 1, 0))
    cw = cw_ref[...]
    conv = prev * cw[0:1, :] + zc * cw[1:2, :] + nxt * cw[2:3, :]
    ya_ref[...] = (gb_ref[...].astype(_F32) * conv).astype(ya_ref.dtype)

    v = _gelu(v_ref[...].astype(_F32))
    mu = jnp.mean(v, axis=-1, keepdims=True)
    vc = v - mu
    var = jnp.mean(vc * vc, axis=-1, keepdims=True)
    vn = (vc * lax.rsqrt(var + EPS) * lng_ref[...] + lnb_ref[...]).astype(_BF16)
    gd = width // groups
    for c in range(tm // chunk):
        r = slice(c * chunk, (c + 1) * chunk)
        for g in range(groups):
            cols = slice(g * gd, (g + 1) * gd)
            mixed = jnp.dot(sw_ref[g], vn[r, cols], preferred_element_type=_F32)
            mixed = mixed + sb_ref[:, cols]
            u = _gelu(u_ref[r, cols].astype(_F32))
            yb_ref[r, cols] = (u * mixed).astype(yb_ref.dtype)


def _mixer(z, conv_w, ln_g, ln_b, sg_w, sg_b, seq_len):
    t = z.shape[0]
    width = conv_w.shape[1]
    groups, chunk, _ = sg_w.shape
    assert ln_g.shape[0] == width, "conv and spatial-gating widths must match"
    tm = _blk(seq_len, 2 * chunk)
    assert tm % chunk == 0 and tm % BF16_SUBLANES == 0
    hb = tm // BF16_SUBLANES
    last_hb = t // BF16_SUBLANES - 1
    gd = width // groups
    sb_full = jnp.repeat(sg_b.T, gd, axis=1)

    def col(c):
        return pl.BlockSpec((tm, width), lambda i: (i, c))

    def halo_prev(c):
        return pl.BlockSpec((BF16_SUBLANES, width),
                            lambda i: (jnp.maximum(i * hb - 1, 0), c))

    def halo_next(c):
        return pl.BlockSpec((BF16_SUBLANES, width),
                            lambda i: (jnp.minimum((i + 1) * hb, last_hb), c))

    def full(shape):
        return pl.BlockSpec(shape, lambda i: (0,) * len(shape))

    kern = functools.partial(_mixer_kernel, seq_len=seq_len, chunk=chunk, groups=groups)
    return pl.pallas_call(
        kern,
        grid=(t // tm,),
        in_specs=[col(0), col(1), col(2), col(3), col(4),
                  halo_prev(0), halo_prev(2), halo_next(0), halo_next(2),
                  full((3, width)), full((1, width)), full((1, width)),
                  full((groups, chunk, chunk)), full((chunk, width))],
        out_specs=[pl.BlockSpec((tm, width), lambda i: (i, 0)),
                   pl.BlockSpec((tm, width), lambda i: (i, 0))],
        out_shape=[jax.ShapeDtypeStruct((t, width), _BF16),
                   jax.ShapeDtypeStruct((t, width), _BF16)],
        compiler_params=_params("parallel"),
        name="mixer",
    )(z, z, z, z, z, z, z, z, z,
      conv_w, ln_g.reshape(1, width), ln_b.reshape(1, width),
      sg_w.astype(_BF16), sb_full)


def _top_rows(sc, k):
    n = sc.shape[0]
    rows = lax.broadcasted_iota(jnp.int32, sc.shape, 0)
    vals, idxs = [], []
    for _ in range(k):
        m = jnp.max(sc, axis=0, keepdims=True)
        idx = jnp.min(jnp.where(sc == m, rows, n), axis=0, keepdims=True)
        sc = jnp.where(rows == idx, -jnp.inf, sc)
        vals.append(m)
        idxs.append(idx)
    return vals, idxs


def _candidate_groups(k):
    return [(k1, k // (k1 + 1)) for k1 in range(k)]


def _query_scores_kernel(x_ref, g_ref, wq_ref, keys_ref, h_ref, sc_ref):
    @pl.when(pl.program_id(1) == 0)
    def _():
        rows = min(x_ref.shape[0], NORM_CHUNK_ROWS)

        def chunk(c, carry):
            r = pl.ds(pl.multiple_of(c * rows, rows), rows)
            _rmsnorm_kernel(x_ref.at[r, :], g_ref, h_ref.at[r, :])
            return carry

        lax.fori_loop(0, x_ref.shape[0] // rows, chunk, 0)

    hk = keys_ref.shape[2]
    q = jnp.dot(h_ref[...], wq_ref[...], preferred_element_type=_F32)
    for hs in range(keys_ref.shape[0]):
        sc_ref[hs] = lax.dot_general(keys_ref[hs], q[:, hs * hk:(hs + 1) * hk],
                                     (((1,), (1,)), ((), ())),
                                     precision=lax.Precision.HIGHEST,
                                     preferred_element_type=_F32)


def _query_scores(x, g, wq, l, keys, bm=512, bn=1024):
    t, d = x.shape
    heads, _, n_keys, hk = keys.shape
    n = heads * 2 * hk
    bm, bn = _blk(t, bm), _blk(n, bn)
    assert bn % hk == 0
    per = bn // hk
    return pl.pallas_call(
        _query_scores_kernel,
        grid=(t // bm, n // bn),
        in_specs=[pl.BlockSpec((bm, d), lambda i, j: (i, 0)),
                  pl.BlockSpec((1, d), lambda i, j: (0, 0)),
                  pl.BlockSpec((None, d, bn), lambda i, j: (l, 0, j)),
                  pl.BlockSpec((per, n_keys, hk), lambda i, j: (j, 0, 0))],
        out_specs=[pl.BlockSpec((bm, d), lambda i, j: (i, 0)),
                   pl.BlockSpec((per, n_keys, bm), lambda i, j: (j, 0, i))],
        out_shape=[jax.ShapeDtypeStruct((t, d), _BF16),
                   jax.ShapeDtypeStruct((heads * 2, n_keys, t), _F32)],
        compiler_params=_params("parallel", "arbitrary"),
        name="query_scores",
    )(x, g.reshape(1, d), wq, keys.reshape(heads * 2, n_keys, hk))


def _retrieve_kernel(sc_ref, i1_ref, i2_ref, g_ref):
    tm = sc_ref.shape[2]
    cw = min(tm, RETRIEVE_CHUNK_LANES)
    for c in range(tm // cw):
        cols = slice(c * cw, (c + 1) * cw)
        i1, i2, g = _retrieve_chunk(sc_ref[0, :, cols], sc_ref[1, :, cols])
        i1_ref[:, cols] = i1
        i2_ref[:, cols] = i2
        g_ref[:, cols] = g


def _retrieve_chunk(sc1, sc2):
    k = PEER_TOPK
    (s1, i1), (s2, i2) = _top_rows(sc1, k), _top_rows(sc2, k)
    cand, c1, c2 = [], [], []
    for k1, n2 in _candidate_groups(k):
        for k2 in range(n2):
            cand.append(s1[k1] + s2[k2])
            c1.append(i1[k1])
            c2.append(i2[k2])
    n_c = len(cand)
    pad = (-n_c) % 8
    lanes = cand[0].shape[1]
    cand = jnp.concatenate(cand + [jnp.full((pad, lanes), -jnp.inf, _F32)], axis=0)
    c1 = jnp.concatenate(c1 + [jnp.zeros((pad, lanes), jnp.int32)], axis=0)
    c2 = jnp.concatenate(c2 + [jnp.zeros((pad, lanes), jnp.int32)], axis=0)
    rows = lax.broadcasted_iota(jnp.int32, cand.shape, 0)
    top, e1, e2 = [], [], []
    for _ in range(k):
        m = jnp.max(cand, axis=0, keepdims=True)
        idx = jnp.min(jnp.where(cand == m, rows, n_c + pad), axis=0, keepdims=True)
        sel = rows == idx
        e1.append(jnp.sum(jnp.where(sel, c1, 0), axis=0, keepdims=True))
        e2.append(jnp.sum(jnp.where(sel, c2, 0), axis=0, keepdims=True))
        cand = jnp.where(sel, -jnp.inf, cand)
        top.append(m)
    top = jnp.concatenate(top, axis=0)
    ex = jnp.exp(top - top[0:1, :])
    gates = ex / jnp.sum(ex, axis=0, keepdims=True)
    return (jnp.concatenate(e1, axis=0).astype(_F32),
            jnp.concatenate(e2, axis=0).astype(_F32), gates)


def _gate_matrix_kernel(i1_ref, i2_ref, g_ref, w_ref, s_ref, *, n_keys):
    tm, slots = i1_ref.shape
    i1 = i1_ref[...]
    i2 = i2_ref[...]
    gt = g_ref[...]
    key_ids = lax.broadcasted_iota(jnp.int32, (n_keys, slots), 0).astype(_F32).astype(_BF16)
    one = jnp.ones((n_keys, slots), _BF16)
    zero = jnp.zeros((n_keys, slots), _BF16)
    grp = BF16_SUBLANES
    n_grp = tm // grp

    ext = n_keys + 2 * BF16_SUBLANES
    ext_ids = (lax.broadcasted_iota(jnp.int32, (ext, slots), 0)
               - W_PITCH_SHIFT).astype(_F32).astype(_BF16)
    ext_one = jnp.ones((ext, slots), _BF16)
    ext_zero = jnp.zeros((ext, slots), _BF16)
    dims = (((1,), (1,)), ((), ()))

    def per_token_products(g):
        for tl in range(grp):
            t = g * grp + tl
            r2 = jnp.broadcast_to(i2[t:t + 1, :].astype(_BF16), (n_keys, slots))
            rg = jnp.broadcast_to(gt[t:t + 1, :].astype(_BF16), (n_keys, slots))
            b_val = jnp.where(key_ids == r2, rg, zero)
            row0 = tl * W_PITCH
            if tl % 2 == 0:
                r1 = jnp.broadcast_to(i1[t:t + 1, :].astype(_BF16), (n_keys, slots))
                a_hot = jnp.where(key_ids == r1, one, zero)
                s_ref[g % 2, row0:row0 + n_keys, :] = lax.dot_general(
                    a_hot, b_val, dims, preferred_element_type=_F32)
            else:
                r1 = jnp.broadcast_to(i1[t:t + 1, :].astype(_BF16), (ext, slots))
                a_hot = jnp.where(ext_ids == r1, ext_one, ext_zero)
                prod = lax.dot_general(a_hot, b_val, dims, preferred_element_type=_F32)
                lo = row0 - W_PITCH_SHIFT
                s_ref[g % 2, lo:lo + n_keys + 8, :] = prod[:n_keys + 8, :]

    def gather_rows(g):
        for a in range(n_keys):
            w_ref[g * grp:(g + 1) * grp, a * n_keys:(a + 1) * n_keys] = (
                s_ref[g % 2, pl.ds(a, grp, stride=W_PITCH), :].astype(w_ref.dtype))

    for g in range(n_grp + 1):
        if g < n_grp:
            per_token_products(g)
        if g > 0:
            gather_rows(g - 1)


def _gate_matrix(i1, i2, gates, n_keys):
    t, slots = i1.shape
    assert n_keys == LANES
    tm = _blk(t, LANES)
    spec = pl.BlockSpec((tm, slots), lambda i: (i, 0))
    return pl.pallas_call(
        functools.partial(_gate_matrix_kernel, n_keys=n_keys),
        grid=(t // tm,),
        in_specs=[spec, spec, spec],
        out_specs=pl.BlockSpec((tm, n_keys * n_keys), lambda i: (i, 0)),
        out_shape=jax.ShapeDtypeStruct((t, n_keys * n_keys), _BF16),
        scratch_shapes=[pltpu.VMEM((2, BF16_SUBLANES * W_PITCH, LANES), _F32)],
        compiler_params=_params("parallel"),
        name="peer_gate_matrix",
    )(i1, i2, gates)


def _experts_kernel(h_ref, ut_ref, v_ref, w_ref, x_hbm, o_ref, x_sem):
    i, n = pl.program_id(0), pl.program_id(1)
    bm = o_ref.shape[0]

    def residual_copy():
        return pltpu.make_async_copy(
            x_hbm.at[pl.ds(pl.multiple_of(i * bm, bm), bm), :], o_ref, x_sem)

    @pl.when(n == 0)
    def _():
        residual_copy().start()

    pre = jnp.dot(h_ref[...], ut_ref[...], preferred_element_type=_F32)
    act = (w_ref[...].astype(_F32) * _gelu(pre)).astype(_BF16)

    @pl.when(n == 0)
    def _():
        residual_copy().wait()

    o_ref[...] += jnp.dot(act, v_ref[...], preferred_element_type=_F32)


def _experts_retrieve_kernel(h_ref, ut_ref, v_ref, w_ref, x_hbm, sc_ref,
                             o_ref, i1_ref, i2_ref, g_ref, x_sem):
    _retrieve_kernel(sc_ref, i1_ref, i2_ref, g_ref)
    _experts_kernel(h_ref, ut_ref, v_ref, w_ref, x_hbm, o_ref, x_sem)


def _experts(x, h, ut, v, l, w, guest=None, bm=1024, bn=512):
    t, d = h.shape
    e = ut.shape[2]
    bm, bn = _blk(t, bm), _blk(e, bn)
    steps_n = e // bn
    in_specs = [pl.BlockSpec((bm, d), lambda i, n: (i, 0)),
                pl.BlockSpec((None, d, bn), lambda i, n: (l, 0, n)),
                pl.BlockSpec((None, bn, d), lambda i, n: (l, n, 0)),
                pl.BlockSpec((bm, bn), lambda i, n: (i, n)),
                pl.BlockSpec(memory_space=pl.ANY)]
    out_spec = pl.BlockSpec((bm, d), lambda i, n: (i, 0), pipeline_mode=pl.Buffered(1))
    out_shape = jax.ShapeDtypeStruct((t, d), _F32)
    scratch = [pltpu.SemaphoreType.DMA(())]
    if guest is None:
        return pl.pallas_call(
            _experts_kernel,
            grid=(t // bm, steps_n),
            in_specs=in_specs,
            out_specs=out_spec,
            out_shape=out_shape,
            scratch_shapes=scratch,
            compiler_params=_params("parallel", "arbitrary"),
            name="peer_experts",
        )(h, ut, v, w, x), None

    hs, n_keys, tg = guest.shape
    heads = hs // 2
    k = PEER_TOPK
    steps = (t // bm) * steps_n
    assert (tg * heads) % steps == 0
    tmg = tg * heads // steps
    assert tmg % LANES == 0 and tg % tmg == 0

    def tile(i, n):
        return (i * steps_n + n) // heads

    def head(i, n):
        return (i * steps_n + n) % heads

    r_shape = jax.ShapeDtypeStruct((heads * k, tg), _F32)
    r_spec = pl.BlockSpec((k, tmg), lambda i, n: (head(i, n), tile(i, n)))
    out, i1, i2, g = pl.pallas_call(
        _experts_retrieve_kernel,
        grid=(t // bm, steps_n),
        in_specs=in_specs + [
            pl.BlockSpec((2, n_keys, tmg), lambda i, n: (head(i, n), 0, tile(i, n)))],
        out_specs=[out_spec, r_spec, r_spec, r_spec],
        out_shape=[out_shape, r_shape, r_shape, r_shape],
        scratch_shapes=scratch,
        compiler_params=_params("parallel", "arbitrary"),
        name="peer_experts_retrieve",
    )(h, ut, v, w, x, guest)
    return out, (i1, i2, g)


def _mixer_block(x, h, p, l, seq_len, scores=None):
    width = p["conv_w"].shape[-1]
    if scores is None:
        z, guest = _mm(h, p["w_in"], l, _BF16), None
    else:
        z, guest = _mm_hosting(h, p["w_in"], l, _BF16, scores)
    ya, yb = _mixer(z, p["conv_w"][l], p["sg_ln_g"][l], p["sg_ln_b"][l],
                    p["sg_w"][l], p["sg_b"][l], seq_len)
    merged = _merge(ya, yb, p["proj_a"], p["proj_b"], l, z, 5 * width)
    x = _mm_res(merged, p["w_o"], l, x)
    h, sc = _query_scores(x, p["norm2_g"][l], p["peer_wq"], l, p["peer_keys"][l])
    return x, h, sc, guest


def _slots_last(picks):
    return tuple(a.T for a in picks)


def _trunk_pair(xa3, xb3, p):
    depth = p["w_in"].shape[0]
    n_keys = p["peer_keys"].shape[-2]
    d = xa3.shape[-1]
    seq_a, seq_b = xa3.shape[1], xb3.shape[1]
    xa, xb = xa3.reshape(-1, d), xb3.reshape(-1, d)
    ha = _rmsnorm(xa, p["norm1_g"][0], _BF16)
    hb = _rmsnorm(xb, p["norm1_g"][0], _BF16)
    xa, ha, sa, _ = _mixer_block(xa, ha, p, 0, seq_a)
    xb, hb, sb, ra = _mixer_block(xb, hb, p, 0, seq_b, scores=sa)
    wa = _gate_matrix(*_slots_last(ra), n_keys)
    for l in range(depth):
        last = l + 1 == depth
        gain, out_dtype = (p["final_g"], _F32) if last else (p["norm1_g"][l + 1], _BF16)
        xa, rb = _experts(xa, ha, p["peer_ut"], p["peer_v"], l, wa, guest=sb)
        ha = _rmsnorm(xa, gain, out_dtype)
        wb = _gate_matrix(*_slots_last(rb), n_keys)
        if last:
            xb, _ = _experts(xb, hb, p["peer_ut"], p["peer_v"], l, wb)
            hb = _rmsnorm(xb, gain, out_dtype)
        else:
            xa, ha, sa, _ = _mixer_block(xa, ha, p, l + 1, seq_a)
            xb, ra = _experts(xb, hb, p["peer_ut"], p["peer_v"], l, wb, guest=sa)
            hb = _rmsnorm(xb, gain, out_dtype)
            xb, hb, sb, _ = _mixer_block(xb, hb, p, l + 1, seq_b)
            wa = _gate_matrix(*_slots_last(ra), n_keys)
    return ha.reshape(xa3.shape), hb.reshape(xb3.shape)


def kernel(x_prompt, x_sample, norm1_g, w_in, conv_w, sg_ln_g, sg_ln_b, sg_w, sg_b, proj_a, proj_b, w_o, norm2_g, peer_wq, peer_keys, peer_u, peer_v, final_g):
    p = dict(
        norm1_g=norm1_g, conv_w=conv_w, sg_ln_g=sg_ln_g, sg_ln_b=sg_ln_b, sg_w=sg_w, sg_b=sg_b,
        norm2_g=norm2_g, peer_keys=peer_keys, final_g=final_g,
        w_in=w_in.astype(_BF16), proj_a=proj_a.astype(_BF16), proj_b=proj_b.astype(_BF16),
        w_o=w_o.astype(_BF16), peer_wq=peer_wq.astype(_BF16),
        peer_ut=jnp.swapaxes(peer_u, 1, 2).astype(_BF16), peer_v=peer_v.astype(_BF16),
    )
    return _trunk_pair(x_prompt, x_sample, p)
```

```python
import functools
import math

import jax
import jax.numpy as jnp
from jax import lax
from jax.experimental import pallas as pl
from jax.experimental.pallas import tpu as pltpu

PEER_TOPK = 16
EPS = 1e-6
V7X_VMEM_LIMIT_BYTES = 56 * 1024 * 1024
BF16_SUBLANES = 16
LANES = 128
MXU_COLUMNS = 256
W_PITCH = 132
W_PITCH_SHIFT = W_PITCH % 8
RETRIEVE_CHUNK_LANES = 256
NORM_CHUNK_ROWS = 64

_F32 = jnp.float32
_BF16 = jnp.bfloat16


def _params(*semantics):
    return pltpu.CompilerParams(dimension_semantics=semantics,
                                vmem_limit_bytes=V7X_VMEM_LIMIT_BYTES)


def _blk(n, target):
    if n <= target:
        return n
    for b in range(target - target % LANES, 0, -LANES):
        if n % b == 0:
            return b
    raise ValueError(f"no lane-aligned block for {n} <= {target}")


def _split_bf16(x):
    hi = x.astype(_BF16)
    return hi, (x - hi.astype(_F32)).astype(_BF16)


def _gelu(x):
    return 0.5 * x * (1.0 + lax.erf(x * (1.0 / math.sqrt(2.0))))


def _rmsnorm_kernel(x_ref, g_ref, h_ref):
    x = x_ref[...]
    ms = jnp.mean(x * x, axis=-1, keepdims=True)
    h_ref[...] = (x * lax.rsqrt(ms + EPS) * g_ref[...]).astype(h_ref.dtype)


def _rmsnorm(x, g, out_dtype):
    t, d = x.shape
    bm = _blk(t, 512)
    return pl.pallas_call(
        _rmsnorm_kernel,
        grid=(t // bm,),
        in_specs=[pl.BlockSpec((bm, d), lambda i: (i, 0)),
                  pl.BlockSpec((1, d), lambda i: (0, 0))],
        out_specs=pl.BlockSpec((bm, d), lambda i: (i, 0)),
        out_shape=jax.ShapeDtypeStruct((t, d), out_dtype),
        compiler_params=_params("parallel"),
        name="rmsnorm",
    )(x, g.reshape(1, d))


def _mm_kernel(x_ref, w_ref, o_ref):
    o_ref[...] = jnp.dot(x_ref[...], w_ref[...],
                         preferred_element_type=_F32).astype(o_ref.dtype)


def _mm(x, w, l, out_dtype, bm=1024, bn=1024):
    m, k = x.shape
    n = w.shape[2]
    bm, bn = _blk(m, bm), _blk(n, bn)
    return pl.pallas_call(
        _mm_kernel,
        grid=(m // bm, n // bn),
        in_specs=[pl.BlockSpec((bm, k), lambda i, j: (i, 0)),
                  pl.BlockSpec((None, k, bn), lambda i, j: (l, 0, j))],
        out_specs=pl.BlockSpec((bm, bn), lambda i, j: (i, j)),
        out_shape=jax.ShapeDtypeStruct((m, n), out_dtype),
        compiler_params=_params("parallel", "parallel"),
        name="matmul",
    )(x, w)


def _mm_retrieve_kernel(x_ref, w_ref, sc_ref, o_ref, i1_ref, i2_ref, g_ref):
    cw = min(sc_ref.shape[2], RETRIEVE_CHUNK_LANES)
    parts = sc_ref.shape[2] // cw
    rows = x_ref.shape[0] // parts

    def part(c, carry):
        lanes = pl.ds(pl.multiple_of(c * cw, cw), cw)
        i1, i2, g = _retrieve_chunk(sc_ref[0, :, lanes], sc_ref[1, :, lanes])
        i1_ref[:, lanes] = i1
        i2_ref[:, lanes] = i2
        g_ref[:, lanes] = g
        r = pl.ds(pl.multiple_of(c * rows, rows), rows)
        o_ref[r, :] = jnp.dot(x_ref[r, :], w_ref[...],
                              preferred_element_type=_F32).astype(o_ref.dtype)
        return carry

    lax.fori_loop(0, parts, part, 0)


def _mm_hosting(x, w, l, out_dtype, scores, bm=512, bn=2304):
    m, k = x.shape
    n = w.shape[2]
    bm, bn = _blk(m, bm), _blk(n, bn)
    steps_m = m // bm
    steps = steps_m * (n // bn)
    hs, keys_n, tg = scores.shape
    heads = hs // 2
    topk = PEER_TOPK
    assert (tg * heads) % steps == 0
    tmg = tg * heads // steps
    assert tmg % LANES == 0 and tg % tmg == 0
    tiles = tg // tmg

    def head(j, i):
        return (j * steps_m + i) // tiles

    def tile(j, i):
        return (j * steps_m + i) % tiles

    r_shape = jax.ShapeDtypeStruct((heads * topk, tg), _F32)
    r_spec = pl.BlockSpec((topk, tmg), lambda j, i: (head(j, i), tile(j, i)))
    out, i1, i2, g = pl.pallas_call(
        _mm_retrieve_kernel,
        grid=(n // bn, steps_m),
        in_specs=[pl.BlockSpec((bm, k), lambda j, i: (i, 0)),
                  pl.BlockSpec((None, k, bn), lambda j, i: (l, 0, j)),
                  pl.BlockSpec((2, keys_n, tmg), lambda j, i: (head(j, i), 0, tile(j, i)))],
        out_specs=[pl.BlockSpec((bm, bn), lambda j, i: (i, j)), r_spec, r_spec, r_spec],
        out_shape=[jax.ShapeDtypeStruct((m, n), out_dtype), r_shape, r_shape, r_shape],
        compiler_params=_params("parallel", "parallel"),
        name="matmul_retrieve",
    )(x, w, scores)
    return out, (i1, i2, g)


def _mm_res_kernel(x_ref, w_ref, r_ref, o_ref):
    o_ref[...] = r_ref[...] + jnp.dot(x_ref[...], w_ref[...], preferred_element_type=_F32)


def _mm_res(x, w, l, res, bm=1024, bn=1024):
    m, k = x.shape
    n = w.shape[2]
    bm, bn = _blk(m, bm), _blk(n, bn)
    return pl.pallas_call(
        _mm_res_kernel,
        grid=(m // bm, n // bn),
        in_specs=[pl.BlockSpec((bm, k), lambda i, j: (i, 0)),
                  pl.BlockSpec((None, k, bn), lambda i, j: (l, 0, j)),
                  pl.BlockSpec((bm, bn), lambda i, j: (i, j))],
        out_specs=pl.BlockSpec((bm, bn), lambda i, j: (i, j)),
        out_shape=jax.ShapeDtypeStruct((m, n), _F32),
        compiler_params=_params("parallel", "parallel"),
        name="matmul_residual",
    )(x, w, res)


def _merge_kernel(ya_ref, yb_ref, pa_ref, pb_ref, ga_ref, gb_ref, o_ref):
    a = jnp.dot(ya_ref[...], pa_ref[...], preferred_element_type=_F32)
    b = jnp.dot(yb_ref[...], pb_ref[...], preferred_element_type=_F32)
    ga = jax.nn.sigmoid(ga_ref[...].astype(_F32))
    gb = jax.nn.sigmoid(gb_ref[...].astype(_F32))
    o_ref[...] = (ga * a + gb * b).astype(o_ref.dtype)


def _merge(ya, yb, pa, pb, l, z, gate_col0, bm=1024, bn=1024):
    m, k = ya.shape
    n = pa.shape[2]
    bm, bn = _blk(m, bm), _blk(math.gcd(n, gate_col0), bn)
    ga0 = gate_col0 // bn
    gb0 = (gate_col0 + n) // bn
    return pl.pallas_call(
        _merge_kernel,
        grid=(m // bm, n // bn),
        in_specs=[pl.BlockSpec((bm, k), lambda i, j: (i, 0)),
                  pl.BlockSpec((bm, k), lambda i, j: (i, 0)),
                  pl.BlockSpec((None, k, bn), lambda i, j: (l, 0, j)),
                  pl.BlockSpec((None, k, bn), lambda i, j: (l, 0, j)),
                  pl.BlockSpec((bm, bn), lambda i, j: (i, ga0 + j)),
                  pl.BlockSpec((bm, bn), lambda i, j: (i, gb0 + j))],
        out_specs=pl.BlockSpec((bm, bn), lambda i, j: (i, j)),
        out_shape=jax.ShapeDtypeStruct((m, n), _BF16),
        compiler_params=_params("parallel", "parallel"),
        name="merge_matmul",
    )(ya, yb, pa, pb, z, z)


def _mixer_kernel(xin_ref, gb_ref, gc_ref, u_ref, v_ref,
                  xin_p_ref, gc_p_ref, xin_n_ref, gc_n_ref,
                  cw_ref, lng_ref, lnb_ref, sw_ref, sb_ref,
                  ya_ref, yb_ref, *, seq_len, chunk, groups):
    i = pl.program_id(0)
    tm, width = xin_ref.shape
    zc = gc_ref[...].astype(_F32) * xin_ref[...].astype(_F32)
    zp = gc_p_ref[...].astype(_F32) * xin_p_ref[...].astype(_F32)
    zn = gc_n_ref[...].astype(_F32) * xin_n_ref[...].astype(_F32)
    row0 = i * tm
    has_prev = (row0 % seq_len != 0).astype(_F32)
    has_next = ((row0 + tm) % seq_len != 0).astype(_F32)
    halo_p = zp[BF16_SUBLANES - 1:BF16_SUBLANES, :] * has_prev
    halo_n = zn[0:1, :] * has_next
    rows = lax.broadcasted_iota(jnp.int32, (tm, width), 0)
    prev = jnp.where(rows == 0, halo_p, pltpu.roll(zc, 1, 0))
    nxt = jnp.where(rows == tm - 1, halo_n, pltpu.roll(zc, tm - 1, 0))
    cw = cw_ref[...]
    conv = prev * cw[0:1, :] + zc * cw[1:2, :] + nxt * cw[2:3, :]
    ya_ref[...] = (gb_ref[...].astype(_F32) * conv).astype(ya_ref.dtype)

    v = _gelu(v_ref[...].astype(_F32))
    mu = jnp.mean(v, axis=-1, keepdims=True)
    vc = v - mu
    var = jnp.mean(vc * vc, axis=-1, keepdims=True)
    vn = (vc * lax.rsqrt(var + EPS) * lng_ref[...] + lnb_ref[...]).astype(_BF16)
    gd = width // groups
    for c in range(tm // chunk):
        r = slice(c * chunk, (c + 1) * chunk)
        for g in range(groups):
            cols = slice(g * gd, (g + 1) * gd)
            mixed = jnp.dot(sw_ref[g], vn[r, cols], preferred_element_type=_F32)
            mixed = mixed + sb_ref[:, cols]
            u = _gelu(u_ref[r, cols].astype(_F32))
            yb_ref[r, cols] = (u * mixed).astype(yb_ref.dtype)


def _mixer(z, conv_w, ln_g, ln_b, sg_w, sg_b, seq_len):
    t = z.shape[0]
    width = conv_w.shape[1]
    groups, chunk, _ = sg_w.shape
    assert ln_g.shape[0] == width, "conv and spatial-gating widths must match"
    tm = _blk(seq_len, 2 * chunk)
    assert tm % chunk == 0 and tm % BF16_SUBLANES == 0
    hb = tm // BF16_SUBLANES
    last_hb = t // BF16_SUBLANES - 1
    gd = width // groups
    sb_full = jnp.repeat(sg_b.T, gd, axis=1)

    def col(c):
        return pl.BlockSpec((tm, width), lambda i: (i, c))

    def halo_prev(c):
        return pl.BlockSpec((BF16_SUBLANES, width),
                            lambda i: (jnp.maximum(i * hb - 1, 0), c))

    def halo_next(c):
        return pl.BlockSpec((BF16_SUBLANES, width),
                            lambda i: (jnp.minimum((i + 1) * hb, last_hb), c))

    def full(shape):
        return pl.BlockSpec(shape, lambda i: (0,) * len(shape))

    kern = functools.partial(_mixer_kernel, seq_len=seq_len, chunk=chunk, groups=groups)
    return pl.pallas_call(
        kern,
        grid=(t // tm,),
        in_specs=[col(0), col(1), col(2), col(3), col(4),
                  halo_prev(0), halo_prev(2), halo_next(0), halo_next(2),
                  full((3, width)), full((1, width)), full((1, width)),
                  full((groups, chunk, chunk)), full((chunk, width))],
        out_specs=[pl.BlockSpec((tm, width), lambda i: (i, 0)),
                   pl.BlockSpec((tm, width), lambda i: (i, 0))],
        out_shape=[jax.ShapeDtypeStruct((t, width), _BF16),
                   jax.ShapeDtypeStruct((t, width), _BF16)],
        compiler_params=_params("parallel"),
        name="mixer",
    )(z, z, z, z, z, z, z, z, z,
      conv_w, ln_g.reshape(1, width), ln_b.reshape(1, width),
      sg_w.astype(_BF16), sb_full)


def _top_rows(sc, k):
    n = sc.shape[0]
    rows = lax.broadcasted_iota(jnp.int32, sc.shape, 0)
    vals, idxs = [], []
    for _ in range(k):
        m = jnp.max(sc, axis=0, keepdims=True)
        idx = jnp.min(jnp.where(sc == m, rows, n), axis=0, keepdims=True)
        sc = jnp.where(rows == idx, -jnp.inf, sc)
        vals.append(m)
        idxs.append(idx)
    return vals, idxs


def _candidate_groups(k):
    return [(k1, k // (k1 + 1)) for k1 in range(k)]


def _query_scores_kernel(x_ref, g_ref, wq_ref, keys_ref, h_ref, sc_ref):
    @pl.when(pl.program_id(1) == 0)
    def _():
        rows = min(x_ref.shape[0], NORM_CHUNK_ROWS)

        def chunk(c, carry):
            r = pl.ds(pl.multiple_of(c * rows, rows), rows)
            _rmsnorm_kernel(x_ref.at[r, :], g_ref, h_ref.at[r, :])
            return carry

        lax.fori_loop(0, x_ref.shape[0] // rows, chunk, 0)

    hk = keys_ref.shape[2]
    q = jnp.dot(h_ref[...], wq_ref[...], preferred_element_type=_F32)
    q_hi, q_lo = _split_bf16(q)
    nt = (((1,), (1,)), ((), ()))
    for hs in range(keys_ref.shape[0]):
        k_hi, k_lo = _split_bf16(keys_ref[hs])
        cols = slice(hs * hk, (hs + 1) * hk)
        sc_ref[hs] = (lax.dot_general(k_hi, q_hi[:, cols], nt, preferred_element_type=_F32)
                      + lax.dot_general(k_hi, q_lo[:, cols], nt, preferred_element_type=_F32)
                      + lax.dot_general(k_lo, q_hi[:, cols], nt, preferred_element_type=_F32))


def _query_scores(x, g, wq, l, keys, bm=512, bn=1024):
    t, d = x.shape
    heads, _, n_keys, hk = keys.shape
    n = heads * 2 * hk
    bm, bn = _blk(t, bm), _blk(n, bn)
    assert bn % hk == 0
    per = bn // hk
    return pl.pallas_call(
        _query_scores_kernel,
        grid=(t // bm, n // bn),
        in_specs=[pl.BlockSpec((bm, d), lambda i, j: (i, 0)),
                  pl.BlockSpec((1, d), lambda i, j: (0, 0)),
                  pl.BlockSpec((None, d, bn), lambda i, j: (l, 0, j)),
                  pl.BlockSpec((per, n_keys, hk), lambda i, j: (j, 0, 0))],
        out_specs=[pl.BlockSpec((bm, d), lambda i, j: (i, 0)),
                   pl.BlockSpec((per, n_keys, bm), lambda i, j: (j, 0, i))],
        out_shape=[jax.ShapeDtypeStruct((t, d), _BF16),
                   jax.ShapeDtypeStruct((heads * 2, n_keys, t), _F32)],
        compiler_params=_params("parallel", "arbitrary"),
        name="query_scores",
    )(x, g.reshape(1, d), wq, keys.reshape(heads * 2, n_keys, hk))


def _retrieve_kernel(sc_ref, i1_ref, i2_ref, g_ref):
    tm = sc_ref.shape[2]
    cw = min(tm, RETRIEVE_CHUNK_LANES)
    for c in range(tm // cw):
        cols = slice(c * cw, (c + 1) * cw)
        i1, i2, g = _retrieve_chunk(sc_ref[0, :, cols], sc_ref[1, :, cols])
        i1_ref[:, cols] = i1
        i2_ref[:, cols] = i2
        g_ref[:, cols] = g


def _retrieve_chunk(sc1, sc2):
    k = PEER_TOPK
    (s1, i1), (s2, i2) = _top_rows(sc1, k), _top_rows(sc2, k)
    cand, c1, c2 = [], [], []
    for k1, n2 in _candidate_groups(k):
        for k2 in range(n2):
            cand.append(s1[k1] + s2[k2])
            c1.append(i1[k1])
            c2.append(i2[k2])
    n_c = len(cand)
    pad = (-n_c) % 8
    lanes = cand[0].shape[1]
    cand = jnp.concatenate(cand + [jnp.full((pad, lanes), -jnp.inf, _F32)], axis=0)
    c1 = jnp.concatenate(c1 + [jnp.zeros((pad, lanes), jnp.int32)], axis=0)
    c2 = jnp.concatenate(c2 + [jnp.zeros((pad, lanes), jnp.int32)], axis=0)
    rows = lax.broadcasted_iota(jnp.int32, cand.shape, 0)
    top, e1, e2 = [], [], []
    for _ in range(k):
        m = jnp.max(cand, axis=0, keepdims=True)
        idx = jnp.min(jnp.where(cand == m, rows, n_c + pad), axis=0, keepdims=True)
        sel = rows == idx
        e1.append(jnp.sum(jnp.where(sel, c1, 0), axis=0, keepdims=True))
        e2.append(jnp.sum(jnp.where(sel, c2, 0), axis=0, keepdims=True))
        cand = jnp.where(sel, -jnp.inf, cand)
        top.append(m)
    top = jnp.concatenate(top, axis=0)
    ex = jnp.exp(top - top[0:1, :])
    gates = ex / jnp.sum(ex, axis=0, keepdims=True)
    return (jnp.concatenate(e1, axis=0).astype(_F32),
            jnp.concatenate(e2, axis=0).astype(_F32), gates)


def _gate_matrix_kernel(i1_ref, i2_ref, g_ref, w_ref, s_ref, *, n_keys):
    tm, slots = i1_ref.shape
    i1 = i1_ref[...]
    i2 = i2_ref[...]
    gt = g_ref[...]
    key_ids = lax.broadcasted_iota(jnp.int32, (n_keys, slots), 0).astype(_F32).astype(_BF16)
    one = jnp.ones((n_keys, slots), _BF16)
    zero = jnp.zeros((n_keys, slots), _BF16)
    grp = BF16_SUBLANES
    n_grp = tm // grp

    ext = n_keys + 2 * BF16_SUBLANES
    ext_ids = (lax.broadcasted_iota(jnp.int32, (ext, slots), 0)
               - W_PITCH_SHIFT).astype(_F32).astype(_BF16)
    ext_one = jnp.ones((ext, slots), _BF16)
    ext_zero = jnp.zeros((ext, slots), _BF16)
    dims = (((1,), (1,)), ((), ()))

    def per_token_products(g):
        for tl in range(grp):
            t = g * grp + tl
            r2 = jnp.broadcast_to(i2[t:t + 1, :].astype(_BF16), (n_keys, slots))
            rg = jnp.broadcast_to(gt[t:t + 1, :].astype(_BF16), (n_keys, slots))
            b_val = jnp.where(key_ids == r2, rg, zero)
            row0 = tl * W_PITCH
            if tl % 2 == 0:
                r1 = jnp.broadcast_to(i1[t:t + 1, :].astype(_BF16), (n_keys, slots))
                a_hot = jnp.where(key_ids == r1, one, zero)
                s_ref[g % 2, row0:row0 + n_keys, :] = lax.dot_general(
                    a_hot, b_val, dims, preferred_element_type=_F32)
            else:
                r1 = jnp.broadcast_to(i1[t:t + 1, :].astype(_BF16), (ext, slots))
                a_hot = jnp.where(ext_ids == r1, ext_one, ext_zero)
                prod = lax.dot_general(a_hot, b_val, dims, preferred_element_type=_F32)
                lo = row0 - W_PITCH_SHIFT
                s_ref[g % 2, lo:lo + n_keys + 8, :] = prod[:n_keys + 8, :]

    def gather_rows(g):
        for a in range(n_keys):
            w_ref[g * grp:(g + 1) * grp, a * n_keys:(a + 1) * n_keys] = (
                s_ref[g % 2, pl.ds(a, grp, stride=W_PITCH), :].astype(w_ref.dtype))

    for g in range(n_grp + 1):
        if g < n_grp:
            per_token_products(g)
        if g > 0:
            gather_rows(g - 1)


def _gate_matrix(i1, i2, gates, n_keys):
    t, slots = i1.shape
    assert n_keys == LANES
    tm = _blk(t, LANES)
    spec = pl.BlockSpec((tm, slots), lambda i: (i, 0))
    return pl.pallas_call(
        functools.partial(_gate_matrix_kernel, n_keys=n_keys),
        grid=(t // tm,),
        in_specs=[spec, spec, spec],
        out_specs=pl.BlockSpec((tm, n_keys * n_keys), lambda i: (i, 0)),
        out_shape=jax.ShapeDtypeStruct((t, n_keys * n_keys), _BF16),
        scratch_shapes=[pltpu.VMEM((2, BF16_SUBLANES * W_PITCH, LANES), _F32)],
        compiler_params=_params("parallel"),
        name="peer_gate_matrix",
    )(i1, i2, gates)


def _experts_kernel(h_ref, ut_ref, v_ref, w_ref, x_hbm, o_ref, x_sem):
    i, n = pl.program_id(0), pl.program_id(1)
    bm = o_ref.shape[0]

    def residual_copy():
        return pltpu.make_async_copy(
            x_hbm.at[pl.ds(pl.multiple_of(i * bm, bm), bm), :], o_ref, x_sem)

    @pl.when(n == 0)
    def _():
        residual_copy().start()

    bn = ut_ref.shape[0]
    sub = min(bn, MXU_COLUMNS)

    parts = bn // sub
    pres = [lax.dot_general(h_ref[...], ut_ref[c * sub:(c + 1) * sub, :],
                            (((1,), (1,)), ((), ())), preferred_element_type=_F32)
            for c in range(parts)]

    def gated(c):
        return (w_ref[:, c * sub:(c + 1) * sub].astype(_F32) * _gelu(pres[c])).astype(_BF16)

    act = gated(0)

    @pl.when(n == 0)
    def _():
        residual_copy().wait()

    for c in range(parts):
        o_ref[...] += jnp.dot(act, v_ref[c * sub:(c + 1) * sub, :],
                              preferred_element_type=_F32)
        if c + 1 < parts:
            act = gated(c + 1)


def _experts_retrieve_kernel(h_ref, ut_ref, v_ref, w_ref, x_hbm, sc_ref,
                             o_ref, i1_ref, i2_ref, g_ref, x_sem):
    _retrieve_kernel(sc_ref, i1_ref, i2_ref, g_ref)
    _experts_kernel(h_ref, ut_ref, v_ref, w_ref, x_hbm, o_ref, x_sem)


def _experts(x, h, ut, v, l, w, guest=None, bm=1024, bn=512):
    t, d = h.shape
    e = ut.shape[1]
    bm, bn = _blk(t, bm), _blk(e, bn)
    steps_n = e // bn
    in_specs = [pl.BlockSpec((bm, d), lambda i, n: (i, 0)),
                pl.BlockSpec((None, bn, d), lambda i, n: (l, n, 0)),
                pl.BlockSpec((None, bn, d), lambda i, n: (l, n, 0)),
                pl.BlockSpec((bm, bn), lambda i, n: (i, n)),
                pl.BlockSpec(memory_space=pl.ANY)]
    out_spec = pl.BlockSpec((bm, d), lambda i, n: (i, 0), pipeline_mode=pl.Buffered(1))
    out_shape = jax.ShapeDtypeStruct((t, d), _F32)
    scratch = [pltpu.SemaphoreType.DMA(())]
    if guest is None:
        return pl.pallas_call(
            _experts_kernel,
            grid=(t // bm, steps_n),
            in_specs=in_specs,
            out_specs=out_spec,
            out_shape=out_shape,
            scratch_shapes=scratch,
            compiler_params=_params("parallel", "arbitrary"),
            name="peer_experts",
        )(h, ut, v, w, x), None

    hs, n_keys, tg = guest.shape
    heads = hs // 2
    k = PEER_TOPK
    steps = (t // bm) * steps_n
    assert (tg * heads) % steps == 0
    tmg = tg * heads // steps
    assert tmg % LANES == 0 and tg % tmg == 0

    def tile(i, n):
        return (i * steps_n + n) // heads

    def head(i, n):
        return (i * steps_n + n) % heads

    r_shape = jax.ShapeDtypeStruct((heads * k, tg), _F32)
    r_spec = pl.BlockSpec((k, tmg), lambda i, n: (head(i, n), tile(i, n)))
    out, i1, i2, g = pl.pallas_call(
        _experts_retrieve_kernel,
        grid=(t // bm, steps_n),
        in_specs=in_specs + [
            pl.BlockSpec((2, n_keys, tmg), lambda i, n: (head(i, n), 0, tile(i, n)))],
        out_specs=[out_spec, r_spec, r_spec, r_spec],
        out_shape=[out_shape, r_shape, r_shape, r_shape],
        scratch_shapes=scratch,
        compiler_params=_params("parallel", "arbitrary"),
        name="peer_experts_retrieve",
    )(h, ut, v, w, x, guest)
    return out, (i1, i2, g)


def _mixer_block(x, h, p, l, seq_len, scores=None):
    width = p["conv_w"].shape[-1]
    if scores is None:
        z, guest = _mm(h, p["w_in"], l, _BF16), None
    else:
        z, guest = _mm_hosting(h, p["w_in"], l, _BF16, scores)
    ya, yb = _mixer(z, p["conv_w"][l], p["sg_ln_g"][l], p["sg_ln_b"][l],
                    p["sg_w"][l], p["sg_b"][l], seq_len)
    merged = _merge(ya, yb, p["proj_a"], p["proj_b"], l, z, 5 * width)
    x = _mm_res(merged, p["w_o"], l, x)
    h, sc = _query_scores(x, p["norm2_g"][l], p["peer_wq"], l, p["peer_keys"][l])
    return x, h, sc, guest


def _slots_last(picks):
    return tuple(a.T for a in picks)


def _trunk_pair(xa3, xb3, p):
    depth = p["w_in"].shape[0]
    n_keys = p["peer_keys"].shape[-2]
    d = xa3.shape[-1]
    seq_a, seq_b = xa3.shape[1], xb3.shape[1]
    xa, xb = xa3.reshape(-1, d), xb3.reshape(-1, d)
    ha = _rmsnorm(xa, p["norm1_g"][0], _BF16)
    hb = _rmsnorm(xb, p["norm1_g"][0], _BF16)
    xa, ha, sa, _ = _mixer_block(xa, ha, p, 0, seq_a)
    xb, hb, sb, ra = _mixer_block(xb, hb, p, 0, seq_b, scores=sa)
    wa = _gate_matrix(*_slots_last(ra), n_keys)
    for l in range(depth):
        last = l + 1 == depth
        gain, out_dtype = (p["final_g"], _F32) if last else (p["norm1_g"][l + 1], _BF16)
        xa, rb = _experts(xa, ha, p["peer_ut"], p["peer_v"], l, wa, guest=sb)
        ha = _rmsnorm(xa, gain, out_dtype)
        wb = _gate_matrix(*_slots_last(rb), n_keys)
        if last:
            xb, _ = _experts(xb, hb, p["peer_ut"], p["peer_v"], l, wb)
            hb = _rmsnorm(xb, gain, out_dtype)
        else:
            xa, ha, sa, _ = _mixer_block(xa, ha, p, l + 1, seq_a)
            xb, ra = _experts(xb, hb, p["peer_ut"], p["peer_v"], l, wb, guest=sa)
            hb = _rmsnorm(xb, gain, out_dtype)
            xb, hb, sb, _ = _mixer_block(xb, hb, p, l + 1, seq_b)
            wa = _gate_matrix(*_slots_last(ra), n_keys)
    return ha.reshape(xa3.shape), hb.reshape(xb3.shape)


def kernel(x_prompt, x_sample, norm1_g, w_in, conv_w, sg_ln_g, sg_ln_b, sg_w, sg_b, proj_a, proj_b, w_o, norm2_g, peer_wq, peer_keys, peer_u, peer_v, final_g):
    p = dict(
        norm1_g=norm1_g, conv_w=conv_w, sg_ln_g=sg_ln_g, sg_ln_b=sg_ln_b, sg_w=sg_w, sg_b=sg_b,
        norm2_g=norm2_g, peer_keys=peer_keys, final_g=final_g,
        w_in=w_in.astype(_BF16), proj_a=proj_a.astype(_BF16), proj_b=proj_b.astype(_BF16),
        w_o=w_o.astype(_BF16), peer_wq=peer_wq.astype(_BF16),
        peer_ut=peer_u.astype(_BF16), peer_v=peer_v.astype(_BF16),
    )
    return _trunk_pair(x_prompt, x_sample, p)
```

```python
import functools
import math

import jax
import jax.numpy as jnp
from jax import lax
from jax.experimental import pallas as pl
from jax.experimental.pallas import tpu as pltpu

PEER_TOPK = 16
EPS = 1e-6
V7X_VMEM_LIMIT_BYTES = 56 * 1024 * 1024
BF16_SUBLANES = 16
LANES = 128
MXU_COLUMNS = 256
W_PITCH = 132
W_PITCH_SHIFT = W_PITCH % 8
RETRIEVE_CHUNK_LANES = 256
NORM_CHUNK_ROWS = 64

_F32 = jnp.float32
_BF16 = jnp.bfloat16


def _params(*semantics):
    return pltpu.CompilerParams(dimension_semantics=semantics,
                                vmem_limit_bytes=V7X_VMEM_LIMIT_BYTES)


def _blk(n, target):
    if n <= target:
        return n
    for b in range(target - target % LANES, 0, -LANES):
        if n % b == 0:
            return b
    raise ValueError(f"no lane-aligned block for {n} <= {target}")


def _split_bf16(x):
    hi = x.astype(_BF16)
    return hi, (x - hi.astype(_F32)).astype(_BF16)


def _gelu(x):
    return 0.5 * x * (1.0 + lax.erf(x * (1.0 / math.sqrt(2.0))))


def _rmsnorm_kernel(x_ref, g_ref, h_ref):
    x = x_ref[...]
    ms = jnp.mean(x * x, axis=-1, keepdims=True)
    h_ref[...] = (x * lax.rsqrt(ms + EPS) * g_ref[...]).astype(h_ref.dtype)


def _rmsnorm(x, g, out_dtype):
    t, d = x.shape
    bm = _blk(t, 512)
    return pl.pallas_call(
        _rmsnorm_kernel,
        grid=(t // bm,),
        in_specs=[pl.BlockSpec((bm, d), lambda i: (i, 0)),
                  pl.BlockSpec((1, d), lambda i: (0, 0))],
        out_specs=pl.BlockSpec((bm, d), lambda i: (i, 0)),
        out_shape=jax.ShapeDtypeStruct((t, d), out_dtype),
        compiler_params=_params("parallel"),
        name="rmsnorm",
    )(x, g.reshape(1, d))


def _mm_kernel(x_ref, w_ref, o_ref):
    o_ref[...] = jnp.dot(x_ref[...], w_ref[...],
                         preferred_element_type=_F32).astype(o_ref.dtype)


def _mm(x, w, l, out_dtype, bm=1024, bn=1024):
    m, k = x.shape
    n = w.shape[2]
    bm, bn = _blk(m, bm), _blk(n, bn)
    return pl.pallas_call(
        _mm_kernel,
        grid=(m // bm, n // bn),
        in_specs=[pl.BlockSpec((bm, k), lambda i, j: (i, 0)),
                  pl.BlockSpec((None, k, bn), lambda i, j: (l, 0, j))],
        out_specs=pl.BlockSpec((bm, bn), lambda i, j: (i, j)),
        out_shape=jax.ShapeDtypeStruct((m, n), out_dtype),
        compiler_params=_params("parallel", "parallel"),
        name="matmul",
    )(x, w)


def _mm_retrieve_kernel(x_ref, w_ref, sc_ref, o_ref, i1_ref, i2_ref, g_ref):
    cw = min(sc_ref.shape[2], RETRIEVE_CHUNK_LANES)
    parts = sc_ref.shape[2] // cw
    rows = x_ref.shape[0] // parts

    def part(c, carry):
        lanes = pl.ds(pl.multiple_of(c * cw, cw), cw)
        i1, i2, g = _retrieve_chunk(sc_ref[0, :, lanes], sc_ref[1, :, lanes])
        i1_ref[:, lanes] = i1
        i2_ref[:, lanes] = i2
        g_ref[:, lanes] = g
        r = pl.ds(pl.multiple_of(c * rows, rows), rows)
        o_ref[r, :] = jnp.dot(x_ref[r, :], w_ref[...],
                              preferred_element_type=_F32).astype(o_ref.dtype)
        return carry

    lax.fori_loop(0, parts, part, 0)


def _mm_hosting(x, w, l, out_dtype, scores, bm=512, bn=2304):
    m, k = x.shape
    n = w.shape[2]
    bm, bn = _blk(m, bm), _blk(n, bn)
    steps_m = m // bm
    steps = steps_m * (n // bn)
    hs, keys_n, tg = scores.shape
    heads = hs // 2
    topk = PEER_TOPK
    assert (tg * heads) % steps == 0
    tmg = tg * heads // steps
    assert tmg % LANES == 0 and tg % tmg == 0
    tiles = tg // tmg

    def head(j, i):
        return (j * steps_m + i) // tiles

    def tile(j, i):
        return (j * steps_m + i) % tiles

    r_shape = jax.ShapeDtypeStruct((heads * topk, tg), _F32)
    r_spec = pl.BlockSpec((topk, tmg), lambda j, i: (head(j, i), tile(j, i)))
    out, i1, i2, g = pl.pallas_call(
        _mm_retrieve_kernel,
        grid=(n // bn, steps_m),
        in_specs=[pl.BlockSpec((bm, k), lambda j, i: (i, 0)),
                  pl.BlockSpec((None, k, bn), lambda j, i: (l, 0, j)),
                  pl.BlockSpec((2, keys_n, tmg), lambda j, i: (head(j, i), 0, tile(j, i)))],
        out_specs=[pl.BlockSpec((bm, bn), lambda j, i: (i, j)), r_spec, r_spec, r_spec],
        out_shape=[jax.ShapeDtypeStruct((m, n), out_dtype), r_shape, r_shape, r_shape],
        compiler_params=_params("parallel", "parallel"),
        name="matmul_retrieve",
    )(x, w, scores)
    return out, (i1, i2, g)


def _mm_res_kernel(x_ref, w_ref, r_ref, o_ref):
    o_ref[...] = r_ref[...] + jnp.dot(x_ref[...], w_ref[...], preferred_element_type=_F32)


def _mm_res(x, w, l, res, bm=1024, bn=1024):
    m, k = x.shape
    n = w.shape[2]
    bm, bn = _blk(m, bm), _blk(n, bn)
    return pl.pallas_call(
        _mm_res_kernel,
        grid=(m // bm, n // bn),
        in_specs=[pl.BlockSpec((bm, k), lambda i, j: (i, 0)),
                  pl.BlockSpec((None, k, bn), lambda i, j: (l, 0, j)),
                  pl.BlockSpec((bm, bn), lambda i, j: (i, j))],
        out_specs=pl.BlockSpec((bm, bn), lambda i, j: (i, j)),
        out_shape=jax.ShapeDtypeStruct((m, n), _F32),
        compiler_params=_params("parallel", "parallel"),
        name="matmul_residual",
    )(x, w, res)


def _merge_kernel(ya_ref, yb_ref, pa_ref, pb_ref, ga_ref, gb_ref, o_ref):
    a = jnp.dot(ya_ref[...], pa_ref[...], preferred_element_type=_F32)
    b = jnp.dot(yb_ref[...], pb_ref[...], preferred_element_type=_F32)
    ga = jax.nn.sigmoid(ga_ref[...].astype(_F32))
    gb = jax.nn.sigmoid(gb_ref[...].astype(_F32))
    o_ref[...] = (ga * a + gb * b).astype(o_ref.dtype)


def _merge(ya, yb, pa, pb, l, z, gate_col0, bm=1024, bn=1024):
    m, k = ya.shape
    n = pa.shape[2]
    bm, bn = _blk(m, bm), _blk(math.gcd(n, gate_col0), bn)
    ga0 = gate_col0 // bn
    gb0 = (gate_col0 + n) // bn
    return pl.pallas_call(
        _merge_kernel,
        grid=(m // bm, n // bn),
        in_specs=[pl.BlockSpec((bm, k), lambda i, j: (i, 0)),
                  pl.BlockSpec((bm, k), lambda i, j: (i, 0)),
                  pl.BlockSpec((None, k, bn), lambda i, j: (l, 0, j)),
                  pl.BlockSpec((None, k, bn), lambda i, j: (l, 0, j)),
                  pl.BlockSpec((bm, bn), lambda i, j: (i, ga0 + j)),
                  pl.BlockSpec((bm, bn), lambda i, j: (i, gb0 + j))],
        out_specs=pl.BlockSpec((bm, bn), lambda i, j: (i, j)),
        out_shape=jax.ShapeDtypeStruct((m, n), _BF16),
        compiler_params=_params("parallel", "parallel"),
        name="merge_matmul",
    )(ya, yb, pa, pb, z, z)


def _mixer_kernel(xin_ref, gb_ref, gc_ref, u_ref, v_ref,
                  xin_p_ref, gc_p_ref, xin_n_ref, gc_n_ref,
                  cw_ref, lng_ref, lnb_ref, sw_ref, sb_ref,
                  ya_ref, yb_ref, *, seq_len, chunk, groups):
    i = pl.program_id(0)
    tm, width = xin_ref.shape
    zc = gc_ref[...].astype(_F32) * xin_ref[...].astype(_F32)
    zp = gc_p_ref[...].astype(_F32) * xin_p_ref[...].astype(_F32)
    zn = gc_n_ref[...].astype(_F32) * xin_n_ref[...].astype(_F32)
    row0 = i * tm
    has_prev = (row0 % seq_len != 0).astype(_F32)
    has_next = ((row0 + tm) % seq_len != 0).astype(_F32)
    halo_p = zp[BF16_SUBLANES - 1:BF16_SUBLANES, :] * has_prev
    halo_n = zn[0:1, :] * has_next
    rows = lax.broadcasted_iota(jnp.int32, (tm, width), 0)
    prev = jnp.where(rows == 0, halo_p, pltpu.roll(zc, 1, 0))
    nxt = jnp.where(rows == tm - 1, halo_n, pltpu.roll(zc, tm - 1, 0))
    cw = cw_ref[...]
    conv = prev * cw[0:1, :] + zc * cw[1:2, :] + nxt * cw[2:3, :]
    ya_ref[...] = (gb_ref[...].astype(_F32) * conv).astype(ya_ref.dtype)

    v = _gelu(v_ref[...].astype(_F32))
    mu = jnp.mean(v, axis=-1, keepdims=True)
    vc = v - mu
    var = jnp.mean(vc * vc, axis=-1, keepdims=True)
    vn = (vc * lax.rsqrt(var + EPS) * lng_ref[...] + lnb_ref[...]).astype(_BF16)
    gd = width // groups
    for c in range(tm // chunk):
        r = slice(c * chunk, (c + 1) * chunk)
        for g in range(groups):
            cols = slice(g * gd, (g + 1) * gd)
            mixed = jnp.dot(sw_ref[g], vn[r, cols], preferred_element_type=_F32)
            mixed = mixed + sb_ref[:, cols]
            u = _gelu(u_ref[r, cols].astype(_F32))
            yb_ref[r, cols] = (u * mixed).astype(yb_ref.dtype)


def _mixer(z, conv_w, ln_g, ln_b, sg_w, sg_b, seq_len):
    t = z.shape[0]
    width = conv_w.shape[1]
    groups, chunk, _ = sg_w.shape
    assert ln_g.shape[0] == width, "conv and spatial-gating widths must match"
    tm = _blk(seq_len, 2 * chunk)
    assert tm % chunk == 0 and tm % BF16_SUBLANES == 0
    hb = tm // BF16_SUBLANES
    last_hb = t // BF16_SUBLANES - 1
    gd = width // groups
    sb_full = jnp.repeat(sg_b.T, gd, axis=1)

    def col(c):
        return pl.BlockSpec((tm, width), lambda i: (i, c))

    def halo_prev(c):
        return pl.BlockSpec((BF16_SUBLANES, width),
                            lambda i: (jnp.maximum(i * hb - 1, 0), c))

    def halo_next(c):
        return pl.BlockSpec((BF16_SUBLANES, width),
                            lambda i: (jnp.minimum((i + 1) * hb, last_hb), c))

    def full(shape):
        return pl.BlockSpec(shape, lambda i: (0,) * len(shape))

    kern = functools.partial(_mixer_kernel, seq_len=seq_len, chunk=chunk, groups=groups)
    return pl.pallas_call(
        kern,
        grid=(t // tm,),
        in_specs=[col(0), col(1), col(2), col(3), col(4),
                  halo_prev(0), halo_prev(2), halo_next(0), halo_next(2),
                  full((3, width)), full((1, width)), full((1, width)),
                  full((groups, chunk, chunk)), full((chunk, width))],
        out_specs=[pl.BlockSpec((tm, width), lambda i: (i, 0)),
                   pl.BlockSpec((tm, width), lambda i: (i, 0))],
        out_shape=[jax.ShapeDtypeStruct((t, width), _BF16),
                   jax.ShapeDtypeStruct((t, width), _BF16)],
        compiler_params=_params("parallel"),
        name="mixer",
    )(z, z, z, z, z, z, z, z, z,
      conv_w, ln_g.reshape(1, width), ln_b.reshape(1, width),
      sg_w.astype(_BF16), sb_full)


def _top_rows(sc, k):
    n = sc.shape[0]
    rows = lax.broadcasted_iota(jnp.int32, sc.shape, 0)
    vals, idxs = [], []
    for _ in range(k):
        m = jnp.max(sc, axis=0, keepdims=True)
        idx = jnp.min(jnp.where(sc == m, rows, n), axis=0, keepdims=True)
        sc = jnp.where(rows == idx, -jnp.inf, sc)
        vals.append(m)
        idxs.append(idx)
    return vals, idxs


def _candidate_groups(k):
    return [(k1, k // (k1 + 1)) for k1 in range(k)]


def _query_scores_kernel(x_ref, g_ref, wq_ref, keys_ref, h_ref, sc_ref):
    @pl.when(pl.program_id(1) == 0)
    def _():
        rows = min(x_ref.shape[0], NORM_CHUNK_ROWS)

        def chunk(c, carry):
            r = pl.ds(pl.multiple_of(c * rows, rows), rows)
            _rmsnorm_kernel(x_ref.at[r, :], g_ref, h_ref.at[r, :])
            return carry

        lax.fori_loop(0, x_ref.shape[0] // rows, chunk, 0)

    hk = keys_ref.shape[2]
    q = jnp.dot(h_ref[...], wq_ref[...], preferred_element_type=_F32)
    q_hi, q_lo = _split_bf16(q)
    nt = (((1,), (1,)), ((), ()))
    for hs in range(keys_ref.shape[0]):
        k_hi, k_lo = _split_bf16(keys_ref[hs])
        cols = slice(hs * hk, (hs + 1) * hk)
        sc_ref[hs] = (lax.dot_general(k_hi, q_hi[:, cols], nt, preferred_element_type=_F32)
                      + lax.dot_general(k_hi, q_lo[:, cols], nt, preferred_element_type=_F32)
                      + lax.dot_general(k_lo, q_hi[:, cols], nt, preferred_element_type=_F32))


def _query_scores(x, g, wq, l, keys, bm=512, bn=1024):
    t, d = x.shape
    heads, _, n_keys, hk = keys.shape
    n = heads * 2 * hk
    bm, bn = _blk(t, bm), _blk(n, bn)
    assert bn % hk == 0
    per = bn // hk
    return pl.pallas_call(
        _query_scores_kernel,
        grid=(t // bm, n // bn),
        in_specs=[pl.BlockSpec((bm, d), lambda i, j: (i, 0)),
                  pl.BlockSpec((1, d), lambda i, j: (0, 0)),
                  pl.BlockSpec((None, d, bn), lambda i, j: (l, 0, j)),
                  pl.BlockSpec((per, n_keys, hk), lambda i, j: (j, 0, 0))],
        out_specs=[pl.BlockSpec((bm, d), lambda i, j: (i, 0)),
                   pl.BlockSpec((per, n_keys, bm), lambda i, j: (j, 0, i))],
        out_shape=[jax.ShapeDtypeStruct((t, d), _BF16),
                   jax.ShapeDtypeStruct((heads * 2, n_keys, t), _F32)],
        compiler_params=_params("parallel", "arbitrary"),
        name="query_scores",
    )(x, g.reshape(1, d), wq, keys.reshape(heads * 2, n_keys, hk))


def _retrieve_kernel(sc_ref, i1_ref, i2_ref, g_ref):
    tm = sc_ref.shape[2]
    cw = min(tm, RETRIEVE_CHUNK_LANES)
    for c in range(tm // cw):
        cols = slice(c * cw, (c + 1) * cw)
        i1, i2, g = _retrieve_chunk(sc_ref[0, :, cols], sc_ref[1, :, cols])
        i1_ref[:, cols] = i1
        i2_ref[:, cols] = i2
        g_ref[:, cols] = g


def _retrieve_chunk(sc1, sc2):
    k = PEER_TOPK
    (s1, i1), (s2, i2) = _top_rows(sc1, k), _top_rows(sc2, k)
    cand, c1, c2 = [], [], []
    for k1, n2 in _candidate_groups(k):
        for k2 in range(n2):
            cand.append(s1[k1] + s2[k2])
            c1.append(i1[k1])
            c2.append(i2[k2])
    n_c = len(cand)
    pad = (-n_c) % 8
    lanes = cand[0].shape[1]
    cand = jnp.concatenate(cand + [jnp.full((pad, lanes), -jnp.inf, _F32)], axis=0)
    c1 = jnp.concatenate(c1 + [jnp.zeros((pad, lanes), jnp.int32)], axis=0)
    c2 = jnp.concatenate(c2 + [jnp.zeros((pad, lanes), jnp.int32)], axis=0)
    rows = lax.broadcasted_iota(jnp.int32, cand.shape, 0)
    top, e1, e2 = [], [], []
    for _ in range(k):
        m = jnp.max(cand, axis=0, keepdims=True)
        idx = jnp.min(jnp.where(cand == m, rows, n_c + pad), axis=0, keepdims=True)
        sel = rows == idx
        e1.append(jnp.sum(jnp.where(sel, c1, 0), axis=0, keepdims=True))
        e2.append(jnp.sum(jnp.where(sel, c2, 0), axis=0, keepdims=True))
        cand = jnp.where(sel, -jnp.inf, cand)
        top.append(m)
    top = jnp.concatenate(top, axis=0)
    ex = jnp.exp(top - top[0:1, :])
    gates = ex / jnp.sum(ex, axis=0, keepdims=True)
    return (jnp.concatenate(e1, axis=0).astype(_F32),
            jnp.concatenate(e2, axis=0).astype(_F32), gates)


def _gate_matrix_kernel(i1_ref, i2_ref, g_ref, w_ref, s_ref, *, n_keys):
    tm, slots = i1_ref.shape
    i1 = i1_ref[...]
    i2 = i2_ref[...]
    gt = g_ref[...]
    key_ids = lax.broadcasted_iota(jnp.int32, (n_keys, slots), 0).astype(_F32).astype(_BF16)
    one = jnp.ones((n_keys, slots), _BF16)
    zero = jnp.zeros((n_keys, slots), _BF16)
    grp = BF16_SUBLANES
    n_grp = tm // grp

    ext = n_keys + 2 * BF16_SUBLANES
    ext_ids = (lax.broadcasted_iota(jnp.int32, (ext, slots), 0)
               - W_PITCH_SHIFT).astype(_F32).astype(_BF16)
    ext_one = jnp.ones((ext, slots), _BF16)
    ext_zero = jnp.zeros((ext, slots), _BF16)
    dims = (((1,), (1,)), ((), ()))

    def per_token_products(g):
        for tl in range(grp):
            t = g * grp + tl
            r2 = jnp.broadcast_to(i2[t:t + 1, :].astype(_BF16), (n_keys, slots))
            rg = jnp.broadcast_to(gt[t:t + 1, :].astype(_BF16), (n_keys, slots))
            b_val = jnp.where(key_ids == r2, rg, zero)
            row0 = tl * W_PITCH
            if tl % 2 == 0:
                r1 = jnp.broadcast_to(i1[t:t + 1, :].astype(_BF16), (n_keys, slots))
                a_hot = jnp.where(key_ids == r1, one, zero)
                s_ref[g % 2, row0:row0 + n_keys, :] = lax.dot_general(
                    a_hot, b_val, dims, preferred_element_type=_F32)
            else:
                r1 = jnp.broadcast_to(i1[t:t + 1, :].astype(_BF16), (ext, slots))
                a_hot = jnp.where(ext_ids == r1, ext_one, ext_zero)
                prod = lax.dot_general(a_hot, b_val, dims, preferred_element_type=_F32)
                lo = row0 - W_PITCH_SHIFT
                s_ref[g % 2, lo:lo + n_keys + 8, :] = prod[:n_keys + 8, :]

    def gather_rows(g):
        for a in range(n_keys):
            w_ref[g * grp:(g + 1) * grp, a * n_keys:(a + 1) * n_keys] = (
                s_ref[g % 2, pl.ds(a, grp, stride=W_PITCH), :].astype(w_ref.dtype))

    for g in range(n_grp + 1):
        if g < n_grp:
            per_token_products(g)
        if g > 0:
            gather_rows(g - 1)


def _gate_matrix(i1, i2, gates, n_keys):
    t, slots = i1.shape
    assert n_keys == LANES
    tm = _blk(t, LANES)
    spec = pl.BlockSpec((tm, slots), lambda i: (i, 0))
    return pl.pallas_call(
        functools.partial(_gate_matrix_kernel, n_keys=n_keys),
        grid=(t // tm,),
        in_specs=[spec, spec, spec],
        out_specs=pl.BlockSpec((tm, n_keys * n_keys), lambda i: (i, 0)),
        out_shape=jax.ShapeDtypeStruct((t, n_keys * n_keys), _BF16),
        scratch_shapes=[pltpu.VMEM((2, BF16_SUBLANES * W_PITCH, LANES), _F32)],
        compiler_params=_params("parallel"),
        name="peer_gate_matrix",
    )(i1, i2, gates)


def _experts_kernel(h_ref, u_ref, v_ref, w_ref, x_hbm, o_ref, x_sem):
    i, n = pl.program_id(0), pl.program_id(1)
    bm = o_ref.shape[0]

    def residual_copy():
        return pltpu.make_async_copy(
            x_hbm.at[pl.ds(pl.multiple_of(i * bm, bm), bm), :], o_ref, x_sem)

    @pl.when(n == 0)
    def _():
        residual_copy().start()

    bn = u_ref.shape[0]
    sub = min(bn, MXU_COLUMNS)

    parts = bn // sub
    pres = [lax.dot_general(h_ref[...], u_ref[c * sub:(c + 1) * sub, :],
                            (((1,), (1,)), ((), ())), preferred_element_type=_F32)
            for c in range(parts)]

    def gated(c):
        return (w_ref[:, c * sub:(c + 1) * sub].astype(_F32) * _gelu(pres[c])).astype(_BF16)

    act = gated(0)

    @pl.when(n == 0)
    def _():
        residual_copy().wait()

    for c in range(parts):
        o_ref[...] += jnp.dot(act, v_ref[c * sub:(c + 1) * sub, :],
                              preferred_element_type=_F32)
        if c + 1 < parts:
            act = gated(c + 1)


def _experts_retrieve_kernel(h_ref, u_ref, v_ref, w_ref, x_hbm, sc_ref,
                             o_ref, i1_ref, i2_ref, g_ref, x_sem):
    _retrieve_kernel(sc_ref, i1_ref, i2_ref, g_ref)
    _experts_kernel(h_ref, u_ref, v_ref, w_ref, x_hbm, o_ref, x_sem)


def _experts_retrieve_cast_kernel(h_ref, u_ref, v_ref, w_ref, x_hbm, sc_ref, src_ref,
                                  o_ref, i1_ref, i2_ref, g_ref, dst_ref, x_sem):
    dst_ref[...] = src_ref[...].astype(dst_ref.dtype)
    _experts_retrieve_kernel(h_ref, u_ref, v_ref, w_ref, x_hbm, sc_ref,
                             o_ref, i1_ref, i2_ref, g_ref, x_sem)


def _experts(x, h, u, v, w, guest=None, cast=None, bm=1024, bn=512):
    t, d = h.shape
    e = u.shape[0]
    bm, bn = _blk(t, bm), _blk(e, bn)
    steps_n = e // bn
    steps = (t // bm) * steps_n
    in_specs = [pl.BlockSpec((bm, d), lambda i, n: (i, 0)),
                pl.BlockSpec((bn, d), lambda i, n: (n, 0)),
                pl.BlockSpec((bn, d), lambda i, n: (n, 0)),
                pl.BlockSpec((bm, bn), lambda i, n: (i, n)),
                pl.BlockSpec(memory_space=pl.ANY)]
    out_spec = pl.BlockSpec((bm, d), lambda i, n: (i, 0), pipeline_mode=pl.Buffered(1))
    out_shape = jax.ShapeDtypeStruct((t, d), _F32)
    scratch = [pltpu.SemaphoreType.DMA(())]
    if guest is None:
        assert cast is None
        out = pl.pallas_call(
            _experts_kernel,
            grid=(t // bm, steps_n),
            in_specs=in_specs,
            out_specs=out_spec,
            out_shape=out_shape,
            scratch_shapes=scratch,
            compiler_params=_params("parallel", "arbitrary"),
            name="peer_experts",
        )(h, u, v, w, x)
        return out, None, None

    hs, n_keys, tg = guest.shape
    heads = hs // 2
    k = PEER_TOPK
    assert (tg * heads) % steps == 0
    tmg = tg * heads // steps
    assert tmg % LANES == 0 and tg % tmg == 0

    def tile(i, n):
        return (i * steps_n + n) // heads

    def head(i, n):
        return (i * steps_n + n) % heads

    r_shape = jax.ShapeDtypeStruct((heads * k, tg), _F32)
    r_spec = pl.BlockSpec((k, tmg), lambda i, n: (head(i, n), tile(i, n)))
    in_specs = in_specs + [
        pl.BlockSpec((2, n_keys, tmg), lambda i, n: (head(i, n), 0, tile(i, n)))]
    out_specs = [out_spec, r_spec, r_spec, r_spec]
    out_shapes = [out_shape, r_shape, r_shape, r_shape]
    if cast is None:
        out, i1, i2, g = pl.pallas_call(
            _experts_retrieve_kernel,
            grid=(t // bm, steps_n),
            in_specs=in_specs,
            out_specs=out_specs,
            out_shape=out_shapes,
            scratch_shapes=scratch,
            compiler_params=_params("parallel", "arbitrary"),
            name="peer_experts_retrieve",
        )(h, u, v, w, x, guest)
        return out, (i1, i2, g), None

    table, layer = cast
    rows, cols = table.shape[1:]
    assert rows % steps == 0 and (rows // steps) % BF16_SUBLANES == 0
    slab = rows // steps
    out, i1, i2, g, copy = pl.pallas_call(
        _experts_retrieve_cast_kernel,
        grid=(t // bm, steps_n),
        in_specs=in_specs + [
            pl.BlockSpec((None, slab, cols), lambda i, n: (layer, i * steps_n + n, 0))],
        out_specs=out_specs + [pl.BlockSpec((slab, cols), lambda i, n: (i * steps_n + n, 0))],
        out_shape=out_shapes + [jax.ShapeDtypeStruct((rows, cols), _BF16)],
        scratch_shapes=scratch,
        compiler_params=_params("parallel", "arbitrary"),
        name="peer_experts_retrieve_cast",
    )(h, u, v, w, x, guest, table)
    return out, (i1, i2, g), copy


def _mixer_block(x, h, p, l, seq_len, scores=None):
    width = p["conv_w"].shape[-1]
    if scores is None:
        z, guest = _mm(h, p["w_in"], l, _BF16), None
    else:
        z, guest = _mm_hosting(h, p["w_in"], l, _BF16, scores)
    ya, yb = _mixer(z, p["conv_w"][l], p["sg_ln_g"][l], p["sg_ln_b"][l],
                    p["sg_w"][l], p["sg_b"][l], seq_len)
    merged = _merge(ya, yb, p["proj_a"], p["proj_b"], l, z, 5 * width)
    x = _mm_res(merged, p["w_o"], l, x)
    h, sc = _query_scores(x, p["norm2_g"][l], p["peer_wq"], l, p["peer_keys"][l])
    return x, h, sc, guest


def _slots_last(picks):
    return tuple(a.T for a in picks)


def _trunk_pair(xa3, xb3, p):
    depth = p["w_in"].shape[0]
    n_keys = p["peer_keys"].shape[-2]
    d = xa3.shape[-1]
    seq_a, seq_b = xa3.shape[1], xb3.shape[1]
    xa, xb = xa3.reshape(-1, d), xb3.reshape(-1, d)
    ha = _rmsnorm(xa, p["norm1_g"][0], _BF16)
    hb = _rmsnorm(xb, p["norm1_g"][0], _BF16)
    xa, ha, sa, _ = _mixer_block(xa, ha, p, 0, seq_a)
    xb, hb, sb, ra = _mixer_block(xb, hb, p, 0, seq_b, scores=sa)
    wa = _gate_matrix(*_slots_last(ra), n_keys)
    u, v = p["peer_u"][0].astype(_BF16), p["peer_v"][0].astype(_BF16)
    for l in range(depth):
        last = l + 1 == depth
        gain, out_dtype = (p["final_g"], _F32) if last else (p["norm1_g"][l + 1], _BF16)
        xa, rb, u_next = _experts(xa, ha, u, v, wa, guest=sb,
                                  cast=None if last else (p["peer_u"], l + 1))
        ha = _rmsnorm(xa, gain, out_dtype)
        wb = _gate_matrix(*_slots_last(rb), n_keys)
        if last:
            xb, _, _ = _experts(xb, hb, u, v, wb)
            hb = _rmsnorm(xb, gain, out_dtype)
        else:
            xa, ha, sa, _ = _mixer_block(xa, ha, p, l + 1, seq_a)
            xb, ra, v_next = _experts(xb, hb, u, v, wb, guest=sa, cast=(p["peer_v"], l + 1))
            hb = _rmsnorm(xb, gain, out_dtype)
            xb, hb, sb, _ = _mixer_block(xb, hb, p, l + 1, seq_b)
            wa = _gate_matrix(*_slots_last(ra), n_keys)
            u, v = u_next, v_next
    return ha.reshape(xa3.shape), hb.reshape(xb3.shape)


def kernel(x_prompt, x_sample, norm1_g, w_in, conv_w, sg_ln_g, sg_ln_b, sg_w, sg_b, proj_a, proj_b, w_o, norm2_g, peer_wq, peer_keys, peer_u, peer_v, final_g):
    p = dict(
        norm1_g=norm1_g, conv_w=conv_w, sg_ln_g=sg_ln_g, sg_ln_b=sg_ln_b, sg_w=sg_w, sg_b=sg_b,
        norm2_g=norm2_g, peer_keys=peer_keys, final_g=final_g, peer_u=peer_u, peer_v=peer_v,
        w_in=w_in.astype(_BF16), proj_a=proj_a.astype(_BF16), proj_b=proj_b.astype(_BF16),
        w_o=w_o.astype(_BF16), peer_wq=peer_wq.astype(_BF16),
    )
    return _trunk_pair(x_prompt, x_sample, p)
```

```python
import functools
import math

import jax
import jax.numpy as jnp
from jax import lax
from jax.experimental import pallas as pl
from jax.experimental.pallas import tpu as pltpu

PEER_TOPK = 16
EPS = 1e-6
V7X_VMEM_LIMIT_BYTES = 56 * 1024 * 1024
BF16_SUBLANES = 16
LANES = 128
MXU_COLUMNS = 256
W_PITCH = 132
W_PITCH_SHIFT = W_PITCH % 8
RETRIEVE_CHUNK_LANES = 256
NORM_CHUNK_ROWS = 64

_F32 = jnp.float32
_BF16 = jnp.bfloat16


def _params(*semantics):
    return pltpu.CompilerParams(dimension_semantics=semantics,
                                vmem_limit_bytes=V7X_VMEM_LIMIT_BYTES)


def _blk(n, target):
    if n <= target:
        return n
    for b in range(target - target % LANES, 0, -LANES):
        if n % b == 0:
            return b
    raise ValueError(f"no lane-aligned block for {n} <= {target}")


def _split_bf16(x):
    hi = x.astype(_BF16)
    return hi, (x - hi.astype(_F32)).astype(_BF16)


def _gelu(x):
    return 0.5 * x * (1.0 + lax.erf(x * (1.0 / math.sqrt(2.0))))


def _rmsnorm_kernel(x_ref, g_ref, h_ref):
    x = x_ref[...]
    ms = jnp.mean(x * x, axis=-1, keepdims=True)
    h_ref[...] = (x * lax.rsqrt(ms + EPS) * g_ref[...]).astype(h_ref.dtype)


def _rmsnorm(x, g, out_dtype):
    t, d = x.shape
    bm = _blk(t, 512)
    return pl.pallas_call(
        _rmsnorm_kernel,
        grid=(t // bm,),
        in_specs=[pl.BlockSpec((bm, d), lambda i: (i, 0)),
                  pl.BlockSpec((1, d), lambda i: (0, 0))],
        out_specs=pl.BlockSpec((bm, d), lambda i: (i, 0)),
        out_shape=jax.ShapeDtypeStruct((t, d), out_dtype),
        compiler_params=_params("parallel"),
        name="rmsnorm",
    )(x, g.reshape(1, d))


def _mm_kernel(x_ref, w_ref, o_ref):
    o_ref[...] = jnp.dot(x_ref[...], w_ref[...],
                         preferred_element_type=_F32).astype(o_ref.dtype)


def _mm_cast_kernel(x_ref, w_ref, src_ref, o_ref, dst_ref):
    dst_ref[...] = src_ref[...].astype(dst_ref.dtype)
    _mm_kernel(x_ref, w_ref, o_ref)


def _mm(x, w, l, out_dtype, cast=None, bm=1024, bn=1024):
    m, k = x.shape
    n = w.shape[2]
    bm, bn = _blk(m, bm), _blk(n, bn)
    grid = (m // bm, n // bn)
    in_specs = [pl.BlockSpec((bm, k), lambda i, j: (i, 0)),
                pl.BlockSpec((None, k, bn), lambda i, j: (l, 0, j))]
    out_spec = pl.BlockSpec((bm, bn), lambda i, j: (i, j))
    out_shape = jax.ShapeDtypeStruct((m, n), out_dtype)
    if cast is None:
        return pl.pallas_call(
            _mm_kernel,
            grid=grid,
            in_specs=in_specs,
            out_specs=out_spec,
            out_shape=out_shape,
            compiler_params=_params("parallel", "parallel"),
            name="matmul",
        )(x, w)
    table, layer = cast
    assert table.shape[1:] == w.shape[1:] and k % grid[0] == 0
    rows = k // grid[0]
    assert rows % BF16_SUBLANES == 0
    return pl.pallas_call(
        _mm_cast_kernel,
        grid=grid,
        in_specs=in_specs + [pl.BlockSpec((None, rows, bn), lambda i, j: (layer, i, j))],
        out_specs=[out_spec, pl.BlockSpec((None, rows, bn), lambda i, j: (0, i, j))],
        out_shape=[out_shape, jax.ShapeDtypeStruct((1, k, n), _BF16)],
        compiler_params=_params("parallel", "parallel"),
        name="matmul_cast",
    )(x, w, table)


def _mm_retrieve_kernel(x_ref, w_ref, sc_ref, o_ref, i1_ref, i2_ref, g_ref):
    cw = min(sc_ref.shape[2], RETRIEVE_CHUNK_LANES)
    parts = sc_ref.shape[2] // cw
    rows = x_ref.shape[0] // parts

    def part(c, carry):
        lanes = pl.ds(pl.multiple_of(c * cw, cw), cw)
        i1, i2, g = _retrieve_chunk(sc_ref[0, :, lanes], sc_ref[1, :, lanes])
        i1_ref[:, lanes] = i1
        i2_ref[:, lanes] = i2
        g_ref[:, lanes] = g
        r = pl.ds(pl.multiple_of(c * rows, rows), rows)
        o_ref[r, :] = jnp.dot(x_ref[r, :], w_ref[...],
                              preferred_element_type=_F32).astype(o_ref.dtype)
        return carry

    lax.fori_loop(0, parts, part, 0)


def _mm_hosting(x, w, l, out_dtype, scores, bm=512, bn=2304):
    m, k = x.shape
    n = w.shape[2]
    bm, bn = _blk(m, bm), _blk(n, bn)
    steps_m = m // bm
    steps = steps_m * (n // bn)
    hs, keys_n, tg = scores.shape
    heads = hs // 2
    topk = PEER_TOPK
    assert (tg * heads) % steps == 0
    tmg = tg * heads // steps
    assert tmg % LANES == 0 and tg % tmg == 0
    tiles = tg // tmg

    def head(j, i):
        return (j * steps_m + i) // tiles

    def tile(j, i):
        return (j * steps_m + i) % tiles

    r_shape = jax.ShapeDtypeStruct((heads * topk, tg), _F32)
    r_spec = pl.BlockSpec((topk, tmg), lambda j, i: (head(j, i), tile(j, i)))
    out, i1, i2, g = pl.pallas_call(
        _mm_retrieve_kernel,
        grid=(n // bn, steps_m),
        in_specs=[pl.BlockSpec((bm, k), lambda j, i: (i, 0)),
                  pl.BlockSpec((None, k, bn), lambda j, i: (l, 0, j)),
                  pl.BlockSpec((2, keys_n, tmg), lambda j, i: (head(j, i), 0, tile(j, i)))],
        out_specs=[pl.BlockSpec((bm, bn), lambda j, i: (i, j)), r_spec, r_spec, r_spec],
        out_shape=[jax.ShapeDtypeStruct((m, n), out_dtype), r_shape, r_shape, r_shape],
        compiler_params=_params("parallel", "parallel"),
        name="matmul_retrieve",
    )(x, w, scores)
    return out, (i1, i2, g)


def _mm_res_kernel(x_ref, w_ref, r_ref, o_ref):
    o_ref[...] = r_ref[...] + jnp.dot(x_ref[...], w_ref[...], preferred_element_type=_F32)


def _mm_res(x, w, l, res, bm=1024, bn=1024):
    m, k = x.shape
    n = w.shape[2]
    bm, bn = _blk(m, bm), _blk(n, bn)
    return pl.pallas_call(
        _mm_res_kernel,
        grid=(m // bm, n // bn),
        in_specs=[pl.BlockSpec((bm, k), lambda i, j: (i, 0)),
                  pl.BlockSpec((None, k, bn), lambda i, j: (l, 0, j)),
                  pl.BlockSpec((bm, bn), lambda i, j: (i, j))],
        out_specs=pl.BlockSpec((bm, bn), lambda i, j: (i, j)),
        out_shape=jax.ShapeDtypeStruct((m, n), _F32),
        compiler_params=_params("parallel", "parallel"),
        name="matmul_residual",
    )(x, w, res)


def _merge_kernel(ya_ref, yb_ref, pa_ref, pb_ref, ga_ref, gb_ref, o_ref):
    a = jnp.dot(ya_ref[...], pa_ref[...], preferred_element_type=_F32)
    b = jnp.dot(yb_ref[...], pb_ref[...], preferred_element_type=_F32)
    ga = jax.nn.sigmoid(ga_ref[...].astype(_F32))
    gb = jax.nn.sigmoid(gb_ref[...].astype(_F32))
    o_ref[...] = (ga * a + gb * b).astype(o_ref.dtype)


def _merge(ya, yb, pa, pb, l, z, gate_col0, bm=1024, bn=1024):
    m, k = ya.shape
    n = pa.shape[2]
    bm, bn = _blk(m, bm), _blk(math.gcd(n, gate_col0), bn)
    ga0 = gate_col0 // bn
    gb0 = (gate_col0 + n) // bn
    return pl.pallas_call(
        _merge_kernel,
        grid=(m // bm, n // bn),
        in_specs=[pl.BlockSpec((bm, k), lambda i, j: (i, 0)),
                  pl.BlockSpec((bm, k), lambda i, j: (i, 0)),
                  pl.BlockSpec((None, k, bn), lambda i, j: (l, 0, j)),
                  pl.BlockSpec((None, k, bn), lambda i, j: (l, 0, j)),
                  pl.BlockSpec((bm, bn), lambda i, j: (i, ga0 + j)),
                  pl.BlockSpec((bm, bn), lambda i, j: (i, gb0 + j))],
        out_specs=pl.BlockSpec((bm, bn), lambda i, j: (i, j)),
        out_shape=jax.ShapeDtypeStruct((m, n), _BF16),
        compiler_params=_params("parallel", "parallel"),
        name="merge_matmul",
    )(ya, yb, pa, pb, z, z)


def _mixer_kernel(xin_ref, gb_ref, gc_ref, u_ref, v_ref,
                  xin_p_ref, gc_p_ref, xin_n_ref, gc_n_ref,
                  cw_ref, lng_ref, lnb_ref, sw_ref, sb_ref,
                  ya_ref, yb_ref, *, seq_len, chunk, groups):
    i = pl.program_id(0)
    tm, width = xin_ref.shape
    zc = gc_ref[...].astype(_F32) * xin_ref[...].astype(_F32)
    zp = gc_p_ref[...].astype(_F32) * xin_p_ref[...].astype(_F32)
    zn = gc_n_ref[...].astype(_F32) * xin_n_ref[...].astype(_F32)
    row0 = i * tm
    has_prev = (row0 % seq_len != 0).astype(_F32)
    has_next = ((row0 + tm) % seq_len != 0).astype(_F32)
    halo_p = zp[BF16_SUBLANES - 1:BF16_SUBLANES, :] * has_prev
    halo_n = zn[0:1, :] * has_next
    rows = lax.broadcasted_iota(jnp.int32, (tm, width), 0)
    prev = jnp.where(rows == 0, halo_p, pltpu.roll(zc, 1, 0))
    nxt = jnp.where(rows == tm - 1, halo_n, pltpu.roll(zc, tm - 1, 0))
    cw = cw_ref[...]
    conv = prev * cw[0:1, :] + zc * cw[1:2, :] + nxt * cw[2:3, :]
    ya_ref[...] = (gb_ref[...].astype(_F32) * conv).astype(ya_ref.dtype)

    v = _gelu(v_ref[...].astype(_F32))
    mu = jnp.mean(v, axis=-1, keepdims=True)
    vc = v - mu
    var = jnp.mean(vc * vc, axis=-1, keepdims=True)
    vn = (vc * lax.rsqrt(var + EPS) * lng_ref[...] + lnb_ref[...]).astype(_BF16)
    gd = width // groups
    for c in range(tm // chunk):
        r = slice(c * chunk, (c + 1) * chunk)
        for g in range(groups):
            cols = slice(g * gd, (g + 1) * gd)
            mixed = jnp.dot(sw_ref[g], vn[r, cols], preferred_element_type=_F32)
            mixed = mixed + sb_ref[:, cols]
            u = _gelu(u_ref[r, cols].astype(_F32))
            yb_ref[r, cols] = (u * mixed).astype(yb_ref.dtype)


def _mixer(z, conv_w, ln_g, ln_b, sg_w, sg_b, seq_len):
    t = z.shape[0]
    width = conv_w.shape[1]
    groups, chunk, _ = sg_w.shape
    assert ln_g.shape[0] == width, "conv and spatial-gating widths must match"
    tm = _blk(seq_len, 2 * chunk)
    assert tm % chunk == 0 and tm % BF16_SUBLANES == 0
    hb = tm // BF16_SUBLANES
    last_hb = t // BF16_SUBLANES - 1
    gd = width // groups
    sb_full = jnp.repeat(sg_b.T, gd, axis=1)

    def col(c):
        return pl.BlockSpec((tm, width), lambda i: (i, c))

    def halo_prev(c):
        return pl.BlockSpec((BF16_SUBLANES, width),
                            lambda i: (jnp.maximum(i * hb - 1, 0), c))

    def halo_next(c):
        return pl.BlockSpec((BF16_SUBLANES, width),
                            lambda i: (jnp.minimum((i + 1) * hb, last_hb), c))

    def full(shape):
        return pl.BlockSpec(shape, lambda i: (0,) * len(shape))

    kern = functools.partial(_mixer_kernel, seq_len=seq_len, chunk=chunk, groups=groups)
    return pl.pallas_call(
        kern,
        grid=(t // tm,),
        in_specs=[col(0), col(1), col(2), col(3), col(4),
                  halo_prev(0), halo_prev(2), halo_next(0), halo_next(2),
                  full((3, width)), full((1, width)), full((1, width)),
                  full((groups, chunk, chunk)), full((chunk, width))],
        out_specs=[pl.BlockSpec((tm, width), lambda i: (i, 0)),
                   pl.BlockSpec((tm, width), lambda i: (i, 0))],
        out_shape=[jax.ShapeDtypeStruct((t, width), _BF16),
                   jax.ShapeDtypeStruct((t, width), _BF16)],
        compiler_params=_params("parallel"),
        name="mixer",
    )(z, z, z, z, z, z, z, z, z,
      conv_w, ln_g.reshape(1, width), ln_b.reshape(1, width),
      sg_w.astype(_BF16), sb_full)


def _top_rows(sc, k):
    n = sc.shape[0]
    rows = lax.broadcasted_iota(jnp.int32, sc.shape, 0)
    vals, idxs = [], []
    for _ in range(k):
        m = jnp.max(sc, axis=0, keepdims=True)
        idx = jnp.min(jnp.where(sc == m, rows, n), axis=0, keepdims=True)
        sc = jnp.where(rows == idx, -jnp.inf, sc)
        vals.append(m)
        idxs.append(idx)
    return vals, idxs


def _candidate_groups(k):
    return [(k1, k // (k1 + 1)) for k1 in range(k)]


def _query_scores_kernel(x_ref, g_ref, wq_ref, keys_ref, h_ref, sc_ref):
    @pl.when(pl.program_id(1) == 0)
    def _():
        rows = min(x_ref.shape[0], NORM_CHUNK_ROWS)

        def chunk(c, carry):
            r = pl.ds(pl.multiple_of(c * rows, rows), rows)
            _rmsnorm_kernel(x_ref.at[r, :], g_ref, h_ref.at[r, :])
            return carry

        lax.fori_loop(0, x_ref.shape[0] // rows, chunk, 0)

    hk = keys_ref.shape[2]
    q = jnp.dot(h_ref[...], wq_ref[...], preferred_element_type=_F32)
    q_hi, q_lo = _split_bf16(q)
    nt = (((1,), (1,)), ((), ()))
    for hs in range(keys_ref.shape[0]):
        k_hi, k_lo = _split_bf16(keys_ref[hs])
        cols = slice(hs * hk, (hs + 1) * hk)
        sc_ref[hs] = (lax.dot_general(k_hi, q_hi[:, cols], nt, preferred_element_type=_F32)
                      + lax.dot_general(k_hi, q_lo[:, cols], nt, preferred_element_type=_F32)
                      + lax.dot_general(k_lo, q_hi[:, cols], nt, preferred_element_type=_F32))


def _query_scores(x, g, wq, l, keys, bm=512, bn=1024):
    t, d = x.shape
    heads, _, n_keys, hk = keys.shape
    n = heads * 2 * hk
    bm, bn = _blk(t, bm), _blk(n, bn)
    assert bn % hk == 0
    per = bn // hk
    return pl.pallas_call(
        _query_scores_kernel,
        grid=(t // bm, n // bn),
        in_specs=[pl.BlockSpec((bm, d), lambda i, j: (i, 0)),
                  pl.BlockSpec((1, d), lambda i, j: (0, 0)),
                  pl.BlockSpec((None, d, bn), lambda i, j: (l, 0, j)),
                  pl.BlockSpec((per, n_keys, hk), lambda i, j: (j, 0, 0))],
        out_specs=[pl.BlockSpec((bm, d), lambda i, j: (i, 0)),
                   pl.BlockSpec((per, n_keys, bm), lambda i, j: (j, 0, i))],
        out_shape=[jax.ShapeDtypeStruct((t, d), _BF16),
                   jax.ShapeDtypeStruct((heads * 2, n_keys, t), _F32)],
        compiler_params=_params("parallel", "arbitrary"),
        name="query_scores",
    )(x, g.reshape(1, d), wq, keys.reshape(heads * 2, n_keys, hk))


def _retrieve_kernel(sc_ref, i1_ref, i2_ref, g_ref):
    tm = sc_ref.shape[2]
    cw = min(tm, RETRIEVE_CHUNK_LANES)
    for c in range(tm // cw):
        cols = slice(c * cw, (c + 1) * cw)
        i1, i2, g = _retrieve_chunk(sc_ref[0, :, cols], sc_ref[1, :, cols])
        i1_ref[:, cols] = i1
        i2_ref[:, cols] = i2
        g_ref[:, cols] = g


def _retrieve_chunk(sc1, sc2):
    k = PEER_TOPK
    (s1, i1), (s2, i2) = _top_rows(sc1, k), _top_rows(sc2, k)
    cand, c1, c2 = [], [], []
    for k1, n2 in _candidate_groups(k):
        for k2 in range(n2):
            cand.append(s1[k1] + s2[k2])
            c1.append(i1[k1])
            c2.append(i2[k2])
    n_c = len(cand)
    pad = (-n_c) % 8
    lanes = cand[0].shape[1]
    cand = jnp.concatenate(cand + [jnp.full((pad, lanes), -jnp.inf, _F32)], axis=0)
    c1 = jnp.concatenate(c1 + [jnp.zeros((pad, lanes), jnp.int32)], axis=0)
    c2 = jnp.concatenate(c2 + [jnp.zeros((pad, lanes), jnp.int32)], axis=0)
    rows = lax.broadcasted_iota(jnp.int32, cand.shape, 0)
    top, e1, e2 = [], [], []
    for _ in range(k):
        m = jnp.max(cand, axis=0, keepdims=True)
        idx = jnp.min(jnp.where(cand == m, rows, n_c + pad), axis=0, keepdims=True)
        sel = rows == idx
        e1.append(jnp.sum(jnp.where(sel, c1, 0), axis=0, keepdims=True))
        e2.append(jnp.sum(jnp.where(sel, c2, 0), axis=0, keepdims=True))
        cand = jnp.where(sel, -jnp.inf, cand)
        top.append(m)
    top = jnp.concatenate(top, axis=0)
    ex = jnp.exp(top - top[0:1, :])
    gates = ex / jnp.sum(ex, axis=0, keepdims=True)
    return (jnp.concatenate(e1, axis=0).astype(_F32),
            jnp.concatenate(e2, axis=0).astype(_F32), gates)


def _gate_matrix_kernel(i1_ref, i2_ref, g_ref, w_ref, s_ref, *, n_keys):
    tm, slots = i1_ref.shape
    i1 = i1_ref[...]
    i2 = i2_ref[...]
    gt = g_ref[...]
    key_ids = lax.broadcasted_iota(jnp.int32, (n_keys, slots), 0).astype(_F32).astype(_BF16)
    one = jnp.ones((n_keys, slots), _BF16)
    zero = jnp.zeros((n_keys, slots), _BF16)
    grp = BF16_SUBLANES
    n_grp = tm // grp

    ext = n_keys + 2 * BF16_SUBLANES
    ext_ids = (lax.broadcasted_iota(jnp.int32, (ext, slots), 0)
               - W_PITCH_SHIFT).astype(_F32).astype(_BF16)
    ext_one = jnp.ones((ext, slots), _BF16)
    ext_zero = jnp.zeros((ext, slots), _BF16)
    dims = (((1,), (1,)), ((), ()))

    def per_token_products(g):
        for tl in range(grp):
            t = g * grp + tl
            r2 = jnp.broadcast_to(i2[t:t + 1, :].astype(_BF16), (n_keys, slots))
            rg = jnp.broadcast_to(gt[t:t + 1, :].astype(_BF16), (n_keys, slots))
            b_val = jnp.where(key_ids == r2, rg, zero)
            row0 = tl * W_PITCH
            if tl % 2 == 0:
                r1 = jnp.broadcast_to(i1[t:t + 1, :].astype(_BF16), (n_keys, slots))
                a_hot = jnp.where(key_ids == r1, one, zero)
                s_ref[g % 2, row0:row0 + n_keys, :] = lax.dot_general(
                    a_hot, b_val, dims, preferred_element_type=_F32)
            else:
                r1 = jnp.broadcast_to(i1[t:t + 1, :].astype(_BF16), (ext, slots))
                a_hot = jnp.where(ext_ids == r1, ext_one, ext_zero)
                prod = lax.dot_general(a_hot, b_val, dims, preferred_element_type=_F32)
                lo = row0 - W_PITCH_SHIFT
                s_ref[g % 2, lo:lo + n_keys + 8, :] = prod[:n_keys + 8, :]

    def gather_rows(g):
        for a in range(n_keys):
            w_ref[g * grp:(g + 1) * grp, a * n_keys:(a + 1) * n_keys] = (
                s_ref[g % 2, pl.ds(a, grp, stride=W_PITCH), :].astype(w_ref.dtype))

    for g in range(n_grp + 1):
        if g < n_grp:
            per_token_products(g)
        if g > 0:
            gather_rows(g - 1)


def _gate_matrix(i1, i2, gates, n_keys):
    t, slots = i1.shape
    assert n_keys == LANES
    tm = _blk(t, LANES)
    spec = pl.BlockSpec((tm, slots), lambda i: (i, 0))
    return pl.pallas_call(
        functools.partial(_gate_matrix_kernel, n_keys=n_keys),
        grid=(t // tm,),
        in_specs=[spec, spec, spec],
        out_specs=pl.BlockSpec((tm, n_keys * n_keys), lambda i: (i, 0)),
        out_shape=jax.ShapeDtypeStruct((t, n_keys * n_keys), _BF16),
        scratch_shapes=[pltpu.VMEM((2, BF16_SUBLANES * W_PITCH, LANES), _F32)],
        compiler_params=_params("parallel"),
        name="peer_gate_matrix",
    )(i1, i2, gates)


def _experts_kernel(h_ref, u_ref, v_ref, w_ref, x_hbm, o_ref, x_sem):
    i, n = pl.program_id(0), pl.program_id(1)
    bm = o_ref.shape[0]

    def residual_copy():
        return pltpu.make_async_copy(
            x_hbm.at[pl.ds(pl.multiple_of(i * bm, bm), bm), :], o_ref, x_sem)

    @pl.when(n == 0)
    def _():
        residual_copy().start()

    bn = u_ref.shape[0]
    sub = min(bn, MXU_COLUMNS)

    parts = bn // sub
    pres = [lax.dot_general(h_ref[...], u_ref[c * sub:(c + 1) * sub, :],
                            (((1,), (1,)), ((), ())), preferred_element_type=_F32)
            for c in range(parts)]

    def gated(c):
        return (w_ref[:, c * sub:(c + 1) * sub].astype(_F32) * _gelu(pres[c])).astype(_BF16)

    act = gated(0)

    @pl.when(n == 0)
    def _():
        residual_copy().wait()

    for c in range(parts):
        o_ref[...] += jnp.dot(act, v_ref[c * sub:(c + 1) * sub, :],
                              preferred_element_type=_F32)
        if c + 1 < parts:
            act = gated(c + 1)


def _experts_retrieve_kernel(h_ref, u_ref, v_ref, w_ref, x_hbm, sc_ref,
                             o_ref, i1_ref, i2_ref, g_ref, x_sem):
    _retrieve_kernel(sc_ref, i1_ref, i2_ref, g_ref)
    _experts_kernel(h_ref, u_ref, v_ref, w_ref, x_hbm, o_ref, x_sem)


def _experts_retrieve_cast_kernel(h_ref, u_ref, v_ref, w_ref, x_hbm, sc_ref, src_ref,
                                  o_ref, i1_ref, i2_ref, g_ref, dst_ref, x_sem):
    dst_ref[...] = src_ref[...].astype(dst_ref.dtype)
    _experts_retrieve_kernel(h_ref, u_ref, v_ref, w_ref, x_hbm, sc_ref,
                             o_ref, i1_ref, i2_ref, g_ref, x_sem)


def _experts(x, h, u, v, w, guest=None, cast=None, bm=1024, bn=512):
    t, d = h.shape
    e = u.shape[0]
    bm, bn = _blk(t, bm), _blk(e, bn)
    steps_n = e // bn
    steps = (t // bm) * steps_n
    in_specs = [pl.BlockSpec((bm, d), lambda i, n: (i, 0)),
                pl.BlockSpec((bn, d), lambda i, n: (n, 0)),
                pl.BlockSpec((bn, d), lambda i, n: (n, 0)),
                pl.BlockSpec((bm, bn), lambda i, n: (i, n)),
                pl.BlockSpec(memory_space=pl.ANY)]
    out_spec = pl.BlockSpec((bm, d), lambda i, n: (i, 0), pipeline_mode=pl.Buffered(1))
    out_shape = jax.ShapeDtypeStruct((t, d), _F32)
    scratch = [pltpu.SemaphoreType.DMA(())]
    if guest is None:
        assert cast is None
        out = pl.pallas_call(
            _experts_kernel,
            grid=(t // bm, steps_n),
            in_specs=in_specs,
            out_specs=out_spec,
            out_shape=out_shape,
            scratch_shapes=scratch,
            compiler_params=_params("parallel", "arbitrary"),
            name="peer_experts",
        )(h, u, v, w, x)
        return out, None, None

    hs, n_keys, tg = guest.shape
    heads = hs // 2
    k = PEER_TOPK
    assert (tg * heads) % steps == 0
    tmg = tg * heads // steps
    assert tmg % LANES == 0 and tg % tmg == 0

    def tile(i, n):
        return (i * steps_n + n) // heads

    def head(i, n):
        return (i * steps_n + n) % heads

    r_shape = jax.ShapeDtypeStruct((heads * k, tg), _F32)
    r_spec = pl.BlockSpec((k, tmg), lambda i, n: (head(i, n), tile(i, n)))
    in_specs = in_specs + [
        pl.BlockSpec((2, n_keys, tmg), lambda i, n: (head(i, n), 0, tile(i, n)))]
    out_specs = [out_spec, r_spec, r_spec, r_spec]
    out_shapes = [out_shape, r_shape, r_shape, r_shape]
    if cast is None:
        out, i1, i2, g = pl.pallas_call(
            _experts_retrieve_kernel,
            grid=(t // bm, steps_n),
            in_specs=in_specs,
            out_specs=out_specs,
            out_shape=out_shapes,
            scratch_shapes=scratch,
            compiler_params=_params("parallel", "arbitrary"),
            name="peer_experts_retrieve",
        )(h, u, v, w, x, guest)
        return out, (i1, i2, g), None

    table, layer = cast
    rows, cols = table.shape[1:]
    assert rows % steps == 0 and (rows // steps) % BF16_SUBLANES == 0
    slab = rows // steps
    out, i1, i2, g, copy = pl.pallas_call(
        _experts_retrieve_cast_kernel,
        grid=(t // bm, steps_n),
        in_specs=in_specs + [
            pl.BlockSpec((None, slab, cols), lambda i, n: (layer, i * steps_n + n, 0))],
        out_specs=out_specs + [pl.BlockSpec((slab, cols), lambda i, n: (i * steps_n + n, 0))],
        out_shape=out_shapes + [jax.ShapeDtypeStruct((rows, cols), _BF16)],
        scratch_shapes=scratch,
        compiler_params=_params("parallel", "arbitrary"),
        name="peer_experts_retrieve_cast",
    )(h, u, v, w, x, guest, table)
    return out, (i1, i2, g), copy


def _mixer_block(x, h, p, l, seq_len, w_in, scores=None, cast=None):
    width = p["conv_w"].shape[-1]
    if scores is not None:
        z, guest = _mm_hosting(h, *w_in, _BF16, scores)
    elif cast is not None:
        z, guest = _mm(h, *w_in, _BF16, cast=cast)
    else:
        z, guest = _mm(h, *w_in, _BF16), None
    ya, yb = _mixer(z, p["conv_w"][l], p["sg_ln_g"][l], p["sg_ln_b"][l],
                    p["sg_w"][l], p["sg_b"][l], seq_len)
    merged = _merge(ya, yb, p["proj_a"], p["proj_b"], l, z, 5 * width)
    x = _mm_res(merged, p["w_o"], l, x)
    h, sc = _query_scores(x, p["norm2_g"][l], p["peer_wq"], l, p["peer_keys"][l])
    return x, h, sc, guest


def _slots_last(picks):
    return tuple(a.T for a in picks)


def _trunk_pair(xa3, xb3, p):
    depth = p["w_in"].shape[0]
    n_keys = p["peer_keys"].shape[-2]
    d = xa3.shape[-1]
    seq_a, seq_b = xa3.shape[1], xb3.shape[1]
    xa, xb = xa3.reshape(-1, d), xb3.reshape(-1, d)
    ha = _rmsnorm(xa, p["norm1_g"][0], _BF16)
    hb = _rmsnorm(xb, p["norm1_g"][0], _BF16)
    w_in = (p["w_in"][0:1].astype(_BF16), 0)

    def next_w_in(l):
        return (p["w_in"], l + 1) if l + 1 < depth else None

    xa, ha, sa, w_next = _mixer_block(xa, ha, p, 0, seq_a, w_in, cast=next_w_in(0))
    xb, hb, sb, ra = _mixer_block(xb, hb, p, 0, seq_b, w_in, scores=sa)
    wa = _gate_matrix(*_slots_last(ra), n_keys)
    u, v = p["peer_u"][0].astype(_BF16), p["peer_v"][0].astype(_BF16)
    for l in range(depth):
        last = l + 1 == depth
        gain, out_dtype = (p["final_g"], _F32) if last else (p["norm1_g"][l + 1], _BF16)
        xa, rb, u_next = _experts(xa, ha, u, v, wa, guest=sb,
                                  cast=None if last else (p["peer_u"], l + 1))
        ha = _rmsnorm(xa, gain, out_dtype)
        wb = _gate_matrix(*_slots_last(rb), n_keys)
        if last:
            xb, _, _ = _experts(xb, hb, u, v, wb)
            hb = _rmsnorm(xb, gain, out_dtype)
        else:
            w_in = (w_next, 0)
            xa, ha, sa, w_next = _mixer_block(xa, ha, p, l + 1, seq_a, w_in,
                                              cast=next_w_in(l + 1))
            xb, ra, v_next = _experts(xb, hb, u, v, wb, guest=sa, cast=(p["peer_v"], l + 1))
            hb = _rmsnorm(xb, gain, out_dtype)
            xb, hb, sb, _ = _mixer_block(xb, hb, p, l + 1, seq_b, w_in)
            wa = _gate_matrix(*_slots_last(ra), n_keys)
            u, v = u_next, v_next
    return ha.reshape(xa3.shape), hb.reshape(xb3.shape)


def kernel(x_prompt, x_sample, norm1_g, w_in, conv_w, sg_ln_g, sg_ln_b, sg_w, sg_b, proj_a, proj_b, w_o, norm2_g, peer_wq, peer_keys, peer_u, peer_v, final_g):
    p = dict(
        norm1_g=norm1_g, conv_w=conv_w, sg_ln_g=sg_ln_g, sg_ln_b=sg_ln_b, sg_w=sg_w, sg_b=sg_b,
        norm2_g=norm2_g, peer_keys=peer_keys, final_g=final_g, peer_u=peer_u, peer_v=peer_v,
        w_in=w_in,
        proj_a=proj_a.astype(_BF16), proj_b=proj_b.astype(_BF16),
        w_o=w_o.astype(_BF16), peer_wq=peer_wq.astype(_BF16),
    )
    return _trunk_pair(x_prompt, x_sample, p)
```

```python
import functools
import math

import jax
import jax.numpy as jnp
from jax import lax
from jax.experimental import pallas as pl
from jax.experimental.pallas import tpu as pltpu

PEER_TOPK = 16
EPS = 1e-6
V7X_VMEM_LIMIT_BYTES = 56 * 1024 * 1024
BF16_SUBLANES = 16
LANES = 128
MXU_COLUMNS = 256
W_PITCH = 132
W_PITCH_SHIFT = W_PITCH % 8
RETRIEVE_CHUNK_LANES = 256
NORM_CHUNK_ROWS = 64

_F32 = jnp.float32
_BF16 = jnp.bfloat16


def _params(*semantics):
    return pltpu.CompilerParams(dimension_semantics=semantics,
                                vmem_limit_bytes=V7X_VMEM_LIMIT_BYTES)


def _blk(n, target):
    if n <= target:
        return n
    for b in range(target - target % LANES, 0, -LANES):
        if n % b == 0:
            return b
    raise ValueError(f"no lane-aligned block for {n} <= {target}")


def _split_bf16(x):
    hi = x.astype(_BF16)
    return hi, (x - hi.astype(_F32)).astype(_BF16)


def _gelu(x):
    return 0.5 * x * (1.0 + lax.erf(x * (1.0 / math.sqrt(2.0))))


def _rmsnorm_kernel(x_ref, g_ref, h_ref):
    x = x_ref[...]
    ms = jnp.mean(x * x, axis=-1, keepdims=True)
    h_ref[...] = (x * lax.rsqrt(ms + EPS) * g_ref[...]).astype(h_ref.dtype)


def _rmsnorm(x, g, out_dtype):
    t, d = x.shape
    bm = _blk(t, 512)
    return pl.pallas_call(
        _rmsnorm_kernel,
        grid=(t // bm,),
        in_specs=[pl.BlockSpec((bm, d), lambda i: (i, 0)),
                  pl.BlockSpec((1, d), lambda i: (0, 0))],
        out_specs=pl.BlockSpec((bm, d), lambda i: (i, 0)),
        out_shape=jax.ShapeDtypeStruct((t, d), out_dtype),
        compiler_params=_params("parallel"),
        name="rmsnorm",
    )(x, g.reshape(1, d))


def _mm_kernel(x_ref, w_ref, o_ref):
    o_ref[...] = jnp.dot(x_ref[...], w_ref[...],
                         preferred_element_type=_F32).astype(o_ref.dtype)


def _mm_cast_kernel(x_ref, w_ref, src_ref, o_ref, dst_ref):
    dst_ref[...] = src_ref[...].astype(dst_ref.dtype)
    _mm_kernel(x_ref, w_ref, o_ref)


def _mm(x, w, l, out_dtype, cast=None, bm=1024, bn=1024):
    m, k = x.shape
    n = w.shape[2]
    bm, bn = _blk(m, bm), _blk(n, bn)
    grid = (m // bm, n // bn)
    in_specs = [pl.BlockSpec((bm, k), lambda i, j: (i, 0)),
                pl.BlockSpec((None, k, bn), lambda i, j: (l, 0, j))]
    out_spec = pl.BlockSpec((bm, bn), lambda i, j: (i, j))
    out_shape = jax.ShapeDtypeStruct((m, n), out_dtype)
    if cast is None:
        return pl.pallas_call(
            _mm_kernel,
            grid=grid,
            in_specs=in_specs,
            out_specs=out_spec,
            out_shape=out_shape,
            compiler_params=_params("parallel", "parallel"),
            name="matmul",
        )(x, w)
    table, layer = cast
    assert table.shape[1:] == w.shape[1:] and k % grid[0] == 0
    rows = k // grid[0]
    assert rows % BF16_SUBLANES == 0
    return pl.pallas_call(
        _mm_cast_kernel,
        grid=grid,
        in_specs=in_specs + [pl.BlockSpec((None, rows, bn), lambda i, j: (layer, i, j))],
        out_specs=[out_spec, pl.BlockSpec((None, rows, bn), lambda i, j: (0, i, j))],
        out_shape=[out_shape, jax.ShapeDtypeStruct((1, k, n), _BF16)],
        compiler_params=_params("parallel", "parallel"),
        name="matmul_cast",
    )(x, w, table)


def _mm_retrieve_kernel(x_ref, w_ref, sc_ref, *refs, n_casts):
    srcs, (o_ref, i1_ref, i2_ref, g_ref), dsts = (
        refs[:n_casts], refs[n_casts:n_casts + 4], refs[n_casts + 4:])
    for src_ref, dst_ref in zip(srcs, dsts):
        dst_ref[...] = src_ref[...].astype(dst_ref.dtype)
    cw = min(sc_ref.shape[2], RETRIEVE_CHUNK_LANES)
    parts = sc_ref.shape[2] // cw
    rows = x_ref.shape[0] // parts

    def part(c, carry):
        lanes = pl.ds(pl.multiple_of(c * cw, cw), cw)
        i1, i2, g = _retrieve_chunk(sc_ref[0, :, lanes], sc_ref[1, :, lanes])
        i1_ref[:, lanes] = i1
        i2_ref[:, lanes] = i2
        g_ref[:, lanes] = g
        r = pl.ds(pl.multiple_of(c * rows, rows), rows)
        o_ref[r, :] = jnp.dot(x_ref[r, :], w_ref[...],
                              preferred_element_type=_F32).astype(o_ref.dtype)
        return carry

    lax.fori_loop(0, parts, part, 0)


def _mm_hosting(x, w, l, out_dtype, scores, casts=(), bm=512, bn=2304):
    m, k = x.shape
    n = w.shape[2]
    bm, bn = _blk(m, bm), _blk(n, bn)
    steps_m = m // bm
    steps = steps_m * (n // bn)
    hs, keys_n, tg = scores.shape
    heads = hs // 2
    topk = PEER_TOPK
    assert (tg * heads) % steps == 0
    tmg = tg * heads // steps
    assert tmg % LANES == 0 and tg % tmg == 0
    tiles = tg // tmg

    def head(j, i):
        return (j * steps_m + i) // tiles

    def tile(j, i):
        return (j * steps_m + i) % tiles

    r_shape = jax.ShapeDtypeStruct((heads * topk, tg), _F32)
    r_spec = pl.BlockSpec((topk, tmg), lambda j, i: (head(j, i), tile(j, i)))
    cast_in, cast_out, cast_shape = [], [], []
    for table, layer in casts:
        rows, cols = table.shape[1:]
        assert rows % steps == 0 and (rows // steps) % BF16_SUBLANES == 0
        slab = rows // steps
        cast_in.append(pl.BlockSpec((None, slab, cols),
                                    lambda j, i, layer=layer: (layer, j * steps_m + i, 0)))
        cast_out.append(pl.BlockSpec((slab, cols), lambda j, i: (j * steps_m + i, 0)))
        cast_shape.append(jax.ShapeDtypeStruct((rows, cols), _BF16))
    w_mode = dict(pipeline_mode=pl.Buffered(1)) if casts else {}
    out, i1, i2, g, *copies = pl.pallas_call(
        functools.partial(_mm_retrieve_kernel, n_casts=len(casts)),
        grid=(n // bn, steps_m),
        in_specs=[pl.BlockSpec((bm, k), lambda j, i: (i, 0)),
                  pl.BlockSpec((None, k, bn), lambda j, i: (l, 0, j), **w_mode),
                  pl.BlockSpec((2, keys_n, tmg), lambda j, i: (head(j, i), 0, tile(j, i)))]
        + cast_in,
        out_specs=[pl.BlockSpec((bm, bn), lambda j, i: (i, j)), r_spec, r_spec, r_spec]
        + cast_out,
        out_shape=[jax.ShapeDtypeStruct((m, n), out_dtype), r_shape, r_shape, r_shape]
        + cast_shape,
        compiler_params=_params("parallel", "parallel"),
        name="matmul_retrieve",
    )(x, w, scores, *[table for table, _ in casts])
    return out, (i1, i2, g), copies


def _mm_res_kernel(x_ref, w_ref, r_ref, o_ref):
    o_ref[...] = r_ref[...] + jnp.dot(x_ref[...], w_ref[...], preferred_element_type=_F32)


def _mm_res(x, w, l, res, bm=1024, bn=1024):
    m, k = x.shape
    n = w.shape[2]
    bm, bn = _blk(m, bm), _blk(n, bn)
    return pl.pallas_call(
        _mm_res_kernel,
        grid=(m // bm, n // bn),
        in_specs=[pl.BlockSpec((bm, k), lambda i, j: (i, 0)),
                  pl.BlockSpec((None, k, bn), lambda i, j: (l, 0, j)),
                  pl.BlockSpec((bm, bn), lambda i, j: (i, j))],
        out_specs=pl.BlockSpec((bm, bn), lambda i, j: (i, j)),
        out_shape=jax.ShapeDtypeStruct((m, n), _F32),
        compiler_params=_params("parallel", "parallel"),
        name="matmul_residual",
    )(x, w, res)


def _merge_kernel(ya_ref, yb_ref, pa_ref, pb_ref, ga_ref, gb_ref, o_ref):
    a = jnp.dot(ya_ref[...], pa_ref[...], preferred_element_type=_F32)
    b = jnp.dot(yb_ref[...], pb_ref[...], preferred_element_type=_F32)
    ga = jax.nn.sigmoid(ga_ref[...].astype(_F32))
    gb = jax.nn.sigmoid(gb_ref[...].astype(_F32))
    o_ref[...] = (ga * a + gb * b).astype(o_ref.dtype)


def _merge(ya, yb, pa, pb, l, z, gate_col0, bm=1024, bn=1024):
    m, k = ya.shape
    n = pa.shape[2]
    bm, bn = _blk(m, bm), _blk(math.gcd(n, gate_col0), bn)
    ga0 = gate_col0 // bn
    gb0 = (gate_col0 + n) // bn
    return pl.pallas_call(
        _merge_kernel,
        grid=(m // bm, n // bn),
        in_specs=[pl.BlockSpec((bm, k), lambda i, j: (i, 0)),
                  pl.BlockSpec((bm, k), lambda i, j: (i, 0)),
                  pl.BlockSpec((None, k, bn), lambda i, j: (l, 0, j)),
                  pl.BlockSpec((None, k, bn), lambda i, j: (l, 0, j)),
                  pl.BlockSpec((bm, bn), lambda i, j: (i, ga0 + j)),
                  pl.BlockSpec((bm, bn), lambda i, j: (i, gb0 + j))],
        out_specs=pl.BlockSpec((bm, bn), lambda i, j: (i, j)),
        out_shape=jax.ShapeDtypeStruct((m, n), _BF16),
        compiler_params=_params("parallel", "parallel"),
        name="merge_matmul",
    )(ya, yb, pa, pb, z, z)


def _mixer_kernel(xin_ref, gb_ref, gc_ref, u_ref, v_ref,
                  xin_p_ref, gc_p_ref, xin_n_ref, gc_n_ref,
                  cw_ref, lng_ref, lnb_ref, sw_ref, sb_ref,
                  ya_ref, yb_ref, *, seq_len, chunk, groups):
    i = pl.program_id(0)
    tm, width = xin_ref.shape
    zc = gc_ref[...].astype(_F32) * xin_ref[...].astype(_F32)
    zp = gc_p_ref[...].astype(_F32) * xin_p_ref[...].astype(_F32)
    zn = gc_n_ref[...].astype(_F32) * xin_n_ref[...].astype(_F32)
    row0 = i * tm
    has_prev = (row0 % seq_len != 0).astype(_F32)
    has_next = ((row0 + tm) % seq_len != 0).astype(_F32)
    halo_p = zp[BF16_SUBLANES - 1:BF16_SUBLANES, :] * has_prev
    halo_n = zn[0:1, :] * has_next
    rows = lax.broadcasted_iota(jnp.int32, (tm, width), 0)
    prev = jnp.where(rows == 0, halo_p, pltpu.roll(zc, 1, 0))
    nxt = jnp.where(rows == tm - 1, halo_n, pltpu.roll(zc, tm - 1, 0))
    cw = cw_ref[...]
    conv = prev * cw[0:1, :] + zc * cw[1:2, :] + nxt * cw[2:3, :]
    ya_ref[...] = (gb_ref[...].astype(_F32) * conv).astype(ya_ref.dtype)

    v = _gelu(v_ref[...].astype(_F32))
    mu = jnp.mean(v, axis=-1, keepdims=True)
    vc = v - mu
    var = jnp.mean(vc * vc, axis=-1, keepdims=True)
    vn = (vc * lax.rsqrt(var + EPS) * lng_ref[...] + lnb_ref[...]).astype(_BF16)
    gd = width // groups
    for c in range(tm // chunk):
        r = slice(c * chunk, (c + 1) * chunk)
        for g in range(groups):
            cols = slice(g * gd, (g + 1) * gd)
            mixed = jnp.dot(sw_ref[g], vn[r, cols], preferred_element_type=_F32)
            mixed = mixed + sb_ref[:, cols]
            u = _gelu(u_ref[r, cols].astype(_F32))
            yb_ref[r, cols] = (u * mixed).astype(yb_ref.dtype)


def _mixer(z, conv_w, ln_g, ln_b, sg_w, sg_b, seq_len):
    t = z.shape[0]
    width = conv_w.shape[1]
    groups, chunk, _ = sg_w.shape
    assert ln_g.shape[0] == width, "conv and spatial-gating widths must match"
    tm = _blk(seq_len, 2 * chunk)
    assert tm % chunk == 0 and tm % BF16_SUBLANES == 0
    hb = tm // BF16_SUBLANES
    last_hb = t // BF16_SUBLANES - 1
    gd = width // groups
    sb_full = jnp.repeat(sg_b.T, gd, axis=1)

    def col(c):
        return pl.BlockSpec((tm, width), lambda i: (i, c))

    def halo_prev(c):
        return pl.BlockSpec((BF16_SUBLANES, width),
                            lambda i: (jnp.maximum(i * hb - 1, 0), c))

    def halo_next(c):
        return pl.BlockSpec((BF16_SUBLANES, width),
                            lambda i: (jnp.minimum((i + 1) * hb, last_hb), c))

    def full(shape):
        return pl.BlockSpec(shape, lambda i: (0,) * len(shape))

    kern = functools.partial(_mixer_kernel, seq_len=seq_len, chunk=chunk, groups=groups)
    return pl.pallas_call(
        kern,
        grid=(t // tm,),
        in_specs=[col(0), col(1), col(2), col(3), col(4),
                  halo_prev(0), halo_prev(2), halo_next(0), halo_next(2),
                  full((3, width)), full((1, width)), full((1, width)),
                  full((groups, chunk, chunk)), full((chunk, width))],
        out_specs=[pl.BlockSpec((tm, width), lambda i: (i, 0)),
                   pl.BlockSpec((tm, width), lambda i: (i, 0))],
        out_shape=[jax.ShapeDtypeStruct((t, width), _BF16),
                   jax.ShapeDtypeStruct((t, width), _BF16)],
        compiler_params=_params("parallel"),
        name="mixer",
    )(z, z, z, z, z, z, z, z, z,
      conv_w, ln_g.reshape(1, width), ln_b.reshape(1, width),
      sg_w.astype(_BF16), sb_full)


def _top_rows(sc, k):
    n = sc.shape[0]
    rows = lax.broadcasted_iota(jnp.int32, sc.shape, 0)
    vals, idxs = [], []
    for _ in range(k):
        m = jnp.max(sc, axis=0, keepdims=True)
        idx = jnp.min(jnp.where(sc == m, rows, n), axis=0, keepdims=True)
        sc = jnp.where(rows == idx, -jnp.inf, sc)
        vals.append(m)
        idxs.append(idx)
    return vals, idxs


def _candidate_groups(k):
    return [(k1, k // (k1 + 1)) for k1 in range(k)]


def _query_scores_kernel(x_ref, g_ref, wq_ref, keys_ref, h_ref, sc_ref):
    @pl.when(pl.program_id(1) == 0)
    def _():
        rows = min(x_ref.shape[0], NORM_CHUNK_ROWS)

        def chunk(c, carry):
            r = pl.ds(pl.multiple_of(c * rows, rows), rows)
            _rmsnorm_kernel(x_ref.at[r, :], g_ref, h_ref.at[r, :])
            return carry

        lax.fori_loop(0, x_ref.shape[0] // rows, chunk, 0)

    hk = keys_ref.shape[2]
    q = jnp.dot(h_ref[...], wq_ref[...], preferred_element_type=_F32)
    q_hi, q_lo = _split_bf16(q)
    nt = (((1,), (1,)), ((), ()))
    for hs in range(keys_ref.shape[0]):
        k_hi, k_lo = _split_bf16(keys_ref[hs])
        cols = slice(hs * hk, (hs + 1) * hk)
        sc_ref[hs] = (lax.dot_general(k_hi, q_hi[:, cols], nt, preferred_element_type=_F32)
                      + lax.dot_general(k_hi, q_lo[:, cols], nt, preferred_element_type=_F32)
                      + lax.dot_general(k_lo, q_hi[:, cols], nt, preferred_element_type=_F32))


def _query_scores(x, g, wq, l, keys, bm=512, bn=1024):
    t, d = x.shape
    heads, _, n_keys, hk = keys.shape
    n = heads * 2 * hk
    bm, bn = _blk(t, bm), _blk(n, bn)
    assert bn % hk == 0
    per = bn // hk
    return pl.pallas_call(
        _query_scores_kernel,
        grid=(t // bm, n // bn),
        in_specs=[pl.BlockSpec((bm, d), lambda i, j: (i, 0)),
                  pl.BlockSpec((1, d), lambda i, j: (0, 0)),
                  pl.BlockSpec((None, d, bn), lambda i, j: (l, 0, j)),
                  pl.BlockSpec((per, n_keys, hk), lambda i, j: (j, 0, 0))],
        out_specs=[pl.BlockSpec((bm, d), lambda i, j: (i, 0)),
                   pl.BlockSpec((per, n_keys, bm), lambda i, j: (j, 0, i))],
        out_shape=[jax.ShapeDtypeStruct((t, d), _BF16),
                   jax.ShapeDtypeStruct((heads * 2, n_keys, t), _F32)],
        compiler_params=_params("parallel", "arbitrary"),
        name="query_scores",
    )(x, g.reshape(1, d), wq, keys.reshape(heads * 2, n_keys, hk))


def _retrieve_kernel(sc_ref, i1_ref, i2_ref, g_ref):
    tm = sc_ref.shape[2]
    cw = min(tm, RETRIEVE_CHUNK_LANES)
    for c in range(tm // cw):
        cols = slice(c * cw, (c + 1) * cw)
        i1, i2, g = _retrieve_chunk(sc_ref[0, :, cols], sc_ref[1, :, cols])
        i1_ref[:, cols] = i1
        i2_ref[:, cols] = i2
        g_ref[:, cols] = g


def _retrieve_chunk(sc1, sc2):
    k = PEER_TOPK
    (s1, i1), (s2, i2) = _top_rows(sc1, k), _top_rows(sc2, k)
    cand, c1, c2 = [], [], []
    for k1, n2 in _candidate_groups(k):
        for k2 in range(n2):
            cand.append(s1[k1] + s2[k2])
            c1.append(i1[k1])
            c2.append(i2[k2])
    n_c = len(cand)
    pad = (-n_c) % 8
    lanes = cand[0].shape[1]
    cand = jnp.concatenate(cand + [jnp.full((pad, lanes), -jnp.inf, _F32)], axis=0)
    c1 = jnp.concatenate(c1 + [jnp.zeros((pad, lanes), jnp.int32)], axis=0)
    c2 = jnp.concatenate(c2 + [jnp.zeros((pad, lanes), jnp.int32)], axis=0)
    rows = lax.broadcasted_iota(jnp.int32, cand.shape, 0)
    top, e1, e2 = [], [], []
    for _ in range(k):
        m = jnp.max(cand, axis=0, keepdims=True)
        idx = jnp.min(jnp.where(cand == m, rows, n_c + pad), axis=0, keepdims=True)
        sel = rows == idx
        e1.append(jnp.sum(jnp.where(sel, c1, 0), axis=0, keepdims=True))
        e2.append(jnp.sum(jnp.where(sel, c2, 0), axis=0, keepdims=True))
        cand = jnp.where(sel, -jnp.inf, cand)
        top.append(m)
    top = jnp.concatenate(top, axis=0)
    ex = jnp.exp(top - top[0:1, :])
    gates = ex / jnp.sum(ex, axis=0, keepdims=True)
    return (jnp.concatenate(e1, axis=0).astype(_F32),
            jnp.concatenate(e2, axis=0).astype(_F32), gates)


def _gate_matrix_kernel(i1_ref, i2_ref, g_ref, w_ref, s_ref, *, n_keys):
    tm, slots = i1_ref.shape
    i1 = i1_ref[...]
    i2 = i2_ref[...]
    gt = g_ref[...]
    key_ids = lax.broadcasted_iota(jnp.int32, (n_keys, slots), 0).astype(_F32).astype(_BF16)
    one = jnp.ones((n_keys, slots), _BF16)
    zero = jnp.zeros((n_keys, slots), _BF16)
    grp = BF16_SUBLANES
    n_grp = tm // grp

    ext = n_keys + 2 * BF16_SUBLANES
    ext_ids = (lax.broadcasted_iota(jnp.int32, (ext, slots), 0)
               - W_PITCH_SHIFT).astype(_F32).astype(_BF16)
    ext_one = jnp.ones((ext, slots), _BF16)
    ext_zero = jnp.zeros((ext, slots), _BF16)
    dims = (((1,), (1,)), ((), ()))

    def per_token_products(g):
        for tl in range(grp):
            t = g * grp + tl
            r2 = jnp.broadcast_to(i2[t:t + 1, :].astype(_BF16), (n_keys, slots))
            rg = jnp.broadcast_to(gt[t:t + 1, :].astype(_BF16), (n_keys, slots))
            b_val = jnp.where(key_ids == r2, rg, zero)
            row0 = tl * W_PITCH
            if tl % 2 == 0:
                r1 = jnp.broadcast_to(i1[t:t + 1, :].astype(_BF16), (n_keys, slots))
                a_hot = jnp.where(key_ids == r1, one, zero)
                s_ref[g % 2, row0:row0 + n_keys, :] = lax.dot_general(
                    a_hot, b_val, dims, preferred_element_type=_F32)
            else:
                r1 = jnp.broadcast_to(i1[t:t + 1, :].astype(_BF16), (ext, slots))
                a_hot = jnp.where(ext_ids == r1, ext_one, ext_zero)
                prod = lax.dot_general(a_hot, b_val, dims, preferred_element_type=_F32)
                lo = row0 - W_PITCH_SHIFT
                s_ref[g % 2, lo:lo + n_keys + 8, :] = prod[:n_keys + 8, :]

    def gather_rows(g):
        for a in range(n_keys):
            w_ref[g * grp:(g + 1) * grp, a * n_keys:(a + 1) * n_keys] = (
                s_ref[g % 2, pl.ds(a, grp, stride=W_PITCH), :].astype(w_ref.dtype))

    for g in range(n_grp + 1):
        if g < n_grp:
            per_token_products(g)
        if g > 0:
            gather_rows(g - 1)


def _gate_matrix(i1, i2, gates, n_keys):
    t, slots = i1.shape
    assert n_keys == LANES
    tm = _blk(t, LANES)
    spec = pl.BlockSpec((tm, slots), lambda i: (i, 0))
    return pl.pallas_call(
        functools.partial(_gate_matrix_kernel, n_keys=n_keys),
        grid=(t // tm,),
        in_specs=[spec, spec, spec],
        out_specs=pl.BlockSpec((tm, n_keys * n_keys), lambda i: (i, 0)),
        out_shape=jax.ShapeDtypeStruct((t, n_keys * n_keys), _BF16),
        scratch_shapes=[pltpu.VMEM((2, BF16_SUBLANES * W_PITCH, LANES), _F32)],
        compiler_params=_params("parallel"),
        name="peer_gate_matrix",
    )(i1, i2, gates)


def _experts_kernel(h_ref, u_ref, v_ref, w_ref, x_hbm, o_ref, x_sem):
    i, n = pl.program_id(0), pl.program_id(1)
    bm = o_ref.shape[0]

    def residual_copy():
        return pltpu.make_async_copy(
            x_hbm.at[pl.ds(pl.multiple_of(i * bm, bm), bm), :], o_ref, x_sem)

    @pl.when(n == 0)
    def _():
        residual_copy().start()

    bn = u_ref.shape[0]
    sub = min(bn, MXU_COLUMNS)

    parts = bn // sub
    pres = [lax.dot_general(h_ref[...], u_ref[c * sub:(c + 1) * sub, :],
                            (((1,), (1,)), ((), ())), preferred_element_type=_F32)
            for c in range(parts)]

    def gated(c):
        return (w_ref[:, c * sub:(c + 1) * sub].astype(_F32) * _gelu(pres[c])).astype(_BF16)

    act = gated(0)

    @pl.when(n == 0)
    def _():
        residual_copy().wait()

    for c in range(parts):
        o_ref[...] += jnp.dot(act, v_ref[c * sub:(c + 1) * sub, :],
                              preferred_element_type=_F32)
        if c + 1 < parts:
            act = gated(c + 1)


def _experts_retrieve_kernel(h_ref, u_ref, v_ref, w_ref, x_hbm, sc_ref,
                             o_ref, i1_ref, i2_ref, g_ref, x_sem):
    _retrieve_kernel(sc_ref, i1_ref, i2_ref, g_ref)
    _experts_kernel(h_ref, u_ref, v_ref, w_ref, x_hbm, o_ref, x_sem)


def _experts_retrieve_cast_kernel(h_ref, u_ref, v_ref, w_ref, x_hbm, sc_ref, src_ref,
                                  o_ref, i1_ref, i2_ref, g_ref, dst_ref, x_sem):
    dst_ref[...] = src_ref[...].astype(dst_ref.dtype)
    _experts_retrieve_kernel(h_ref, u_ref, v_ref, w_ref, x_hbm, sc_ref,
                             o_ref, i1_ref, i2_ref, g_ref, x_sem)


def _experts(x, h, u, v, w, guest=None, cast=None, bm=1024, bn=512):
    t, d = h.shape
    e = u.shape[0]
    bm, bn = _blk(t, bm), _blk(e, bn)
    steps_n = e // bn
    steps = (t // bm) * steps_n
    in_specs = [pl.BlockSpec((bm, d), lambda i, n: (i, 0)),
                pl.BlockSpec((bn, d), lambda i, n: (n, 0)),
                pl.BlockSpec((bn, d), lambda i, n: (n, 0)),
                pl.BlockSpec((bm, bn), lambda i, n: (i, n)),
                pl.BlockSpec(memory_space=pl.ANY)]
    out_spec = pl.BlockSpec((bm, d), lambda i, n: (i, 0), pipeline_mode=pl.Buffered(1))
    out_shape = jax.ShapeDtypeStruct((t, d), _F32)
    scratch = [pltpu.SemaphoreType.DMA(())]
    if guest is None:
        assert cast is None
        out = pl.pallas_call(
            _experts_kernel,
            grid=(t // bm, steps_n),
            in_specs=in_specs,
            out_specs=out_spec,
            out_shape=out_shape,
            scratch_shapes=scratch,
            compiler_params=_params("parallel", "arbitrary"),
            name="peer_experts",
        )(h, u, v, w, x)
        return out, None, None

    hs, n_keys, tg = guest.shape
    heads = hs // 2
    k = PEER_TOPK
    assert (tg * heads) % steps == 0
    tmg = tg * heads // steps
    assert tmg % LANES == 0 and tg % tmg == 0

    def tile(i, n):
        return (i * steps_n + n) // heads

    def head(i, n):
        return (i * steps_n + n) % heads

    r_shape = jax.ShapeDtypeStruct((heads * k, tg), _F32)
    r_spec = pl.BlockSpec((k, tmg), lambda i, n: (head(i, n), tile(i, n)))
    in_specs = in_specs + [
        pl.BlockSpec((2, n_keys, tmg), lambda i, n: (head(i, n), 0, tile(i, n)))]
    out_specs = [out_spec, r_spec, r_spec, r_spec]
    out_shapes = [out_shape, r_shape, r_shape, r_shape]
    if cast is None:
        out, i1, i2, g = pl.pallas_call(
            _experts_retrieve_kernel,
            grid=(t // bm, steps_n),
            in_specs=in_specs,
            out_specs=out_specs,
            out_shape=out_shapes,
            scratch_shapes=scratch,
            compiler_params=_params("parallel", "arbitrary"),
            name="peer_experts_retrieve",
        )(h, u, v, w, x, guest)
        return out, (i1, i2, g), None

    table, layer = cast
    rows, cols = table.shape[1:]
    assert rows % steps == 0 and (rows // steps) % BF16_SUBLANES == 0
    slab = rows // steps
    out, i1, i2, g, copy = pl.pallas_call(
        _experts_retrieve_cast_kernel,
        grid=(t // bm, steps_n),
        in_specs=in_specs + [
            pl.BlockSpec((None, slab, cols), lambda i, n: (layer, i * steps_n + n, 0))],
        out_specs=out_specs + [pl.BlockSpec((slab, cols), lambda i, n: (i * steps_n + n, 0))],
        out_shape=out_shapes + [jax.ShapeDtypeStruct((rows, cols), _BF16)],
        scratch_shapes=scratch,
        compiler_params=_params("parallel", "arbitrary"),
        name="peer_experts_retrieve_cast",
    )(h, u, v, w, x, guest, table)
    return out, (i1, i2, g), copy


def _mixer_block(x, h, p, l, seq_len, w_in, scores=None, cast=None, casts=()):
    width = p["conv_w"].shape[-1]
    if scores is not None:
        z, *guest = _mm_hosting(h, *w_in, _BF16, scores, casts)
    elif cast is not None:
        z, guest = _mm(h, *w_in, _BF16, cast=cast)
    else:
        z, guest = _mm(h, *w_in, _BF16), None
    ya, yb = _mixer(z, p["conv_w"][l], p["sg_ln_g"][l], p["sg_ln_b"][l],
                    p["sg_w"][l], p["sg_b"][l], seq_len)
    merged = _merge(ya, yb, p["proj_a"], p["proj_b"], l, z, 5 * width)
    x = _mm_res(merged, p["w_o"], l, x)
    h, sc = _query_scores(x, p["norm2_g"][l], p["peer_wq"], l, p["peer_keys"][l])
    return x, h, sc, guest


def _slots_last(picks):
    return tuple(a.T for a in picks)


def _trunk_pair(xa3, xb3, p):
    depth = p["w_in"].shape[0]
    n_keys = p["peer_keys"].shape[-2]
    d = xa3.shape[-1]
    seq_a, seq_b = xa3.shape[1], xb3.shape[1]
    xa, xb = xa3.reshape(-1, d), xb3.reshape(-1, d)
    ha = _rmsnorm(xa, p["norm1_g"][0], _BF16)
    hb = _rmsnorm(xb, p["norm1_g"][0], _BF16)
    w_in = (p["w_in"][0:1].astype(_BF16), 0)

    def next_w_in(l):
        return (p["w_in"], l + 1) if l + 1 < depth else None

    xa, ha, sa, w_next = _mixer_block(xa, ha, p, 0, seq_a, w_in, cast=next_w_in(0))
    xb, hb, sb, (ra, (u, v)) = _mixer_block(xb, hb, p, 0, seq_b, w_in, scores=sa,
                                            casts=((p["peer_u"], 0), (p["peer_v"], 0)))
    wa = _gate_matrix(*_slots_last(ra), n_keys)
    for l in range(depth):
        last = l + 1 == depth
        gain, out_dtype = (p["final_g"], _F32) if last else (p["norm1_g"][l + 1], _BF16)
        xa, rb, u_next = _experts(xa, ha, u, v, wa, guest=sb,
                                  cast=None if last else (p["peer_u"], l + 1))
        ha = _rmsnorm(xa, gain, out_dtype)
        wb = _gate_matrix(*_slots_last(rb), n_keys)
        if last:
            xb, _, _ = _experts(xb, hb, u, v, wb)
            hb = _rmsnorm(xb, gain, out_dtype)
        else:
            w_in = (w_next, 0)
            xa, ha, sa, w_next = _mixer_block(xa, ha, p, l + 1, seq_a, w_in,
                                              cast=next_w_in(l + 1))
            xb, ra, v_next = _experts(xb, hb, u, v, wb, guest=sa, cast=(p["peer_v"], l + 1))
            hb = _rmsnorm(xb, gain, out_dtype)
            xb, hb, sb, _ = _mixer_block(xb, hb, p, l + 1, seq_b, w_in)
            wa = _gate_matrix(*_slots_last(ra), n_keys)
            u, v = u_next, v_next
    return ha.reshape(xa3.shape), hb.reshape(xb3.shape)


def kernel(x_prompt, x_sample, norm1_g, w_in, conv_w, sg_ln_g, sg_ln_b, sg_w, sg_b, proj_a, proj_b, w_o, norm2_g, peer_wq, peer_keys, peer_u, peer_v, final_g):
    p = dict(
        norm1_g=norm1_g, conv_w=conv_w, sg_ln_g=sg_ln_g, sg_ln_b=sg_ln_b, sg_w=sg_w, sg_b=sg_b,
        norm2_g=norm2_g, peer_keys=peer_keys, final_g=final_g, peer_u=peer_u, peer_v=peer_v,
        w_in=w_in,
        proj_a=proj_a.astype(_BF16), proj_b=proj_b.astype(_BF16),
        w_o=w_o.astype(_BF16), peer_wq=peer_wq.astype(_BF16),
    )
    return _trunk_pair(x_prompt, x_sample, p)
```

```python
import functools
import math

import jax
import jax.numpy as jnp
from jax import lax
from jax.experimental import pallas as pl
from jax.experimental.pallas import tpu as pltpu

PEER_TOPK = 16
EPS = 1e-6
V7X_VMEM_LIMIT_BYTES = 56 * 1024 * 1024
BF16_SUBLANES = 16
LANES = 128
MXU_COLUMNS = 256
W_PITCH = 132
W_PITCH_SHIFT = W_PITCH % 8
RETRIEVE_CHUNK_LANES = 256
NORM_CHUNK_ROWS = 64

_F32 = jnp.float32
_BF16 = jnp.bfloat16


def _params(*semantics):
    return pltpu.CompilerParams(dimension_semantics=semantics,
                                vmem_limit_bytes=V7X_VMEM_LIMIT_BYTES)


def _blk(n, target):
    if n <= target:
        return n
    for b in range(target - target % LANES, 0, -LANES):
        if n % b == 0:
            return b
    raise ValueError(f"no lane-aligned block for {n} <= {target}")


def _split_bf16(x):
    hi = x.astype(_BF16)
    return hi, (x - hi.astype(_F32)).astype(_BF16)


def _gelu(x):
    return 0.5 * x * (1.0 + lax.erf(x * (1.0 / math.sqrt(2.0))))


def _rmsnorm_kernel(x_ref, g_ref, h_ref):
    x = x_ref[...]
    ms = jnp.mean(x * x, axis=-1, keepdims=True)
    h_ref[...] = (x * lax.rsqrt(ms + EPS) * g_ref[...]).astype(h_ref.dtype)


def _rmsnorm(x, g, out_dtype):
    t, d = x.shape
    bm = _blk(t, 512)
    return pl.pallas_call(
        _rmsnorm_kernel,
        grid=(t // bm,),
        in_specs=[pl.BlockSpec((bm, d), lambda i: (i, 0)),
                  pl.BlockSpec((1, d), lambda i: (0, 0))],
        out_specs=pl.BlockSpec((bm, d), lambda i: (i, 0)),
        out_shape=jax.ShapeDtypeStruct((t, d), out_dtype),
        compiler_params=_params("parallel"),
        name="rmsnorm",
    )(x, g.reshape(1, d))


def _mm_kernel(x_ref, w_ref, o_ref):
    o_ref[...] = jnp.dot(x_ref[...], w_ref[...],
                         preferred_element_type=_F32).astype(o_ref.dtype)


def _mm_cast_kernel(x_ref, w_ref, src_ref, o_ref, dst_ref):
    dst_ref[...] = src_ref[...].astype(dst_ref.dtype)
    _mm_kernel(x_ref, w_ref, o_ref)


def _mm(x, w, l, out_dtype, cast=None, bm=1024, bn=1024):
    m, k = x.shape
    n = w.shape[2]
    bm, bn = _blk(m, bm), _blk(n, bn)
    grid = (m // bm, n // bn)
    in_specs = [pl.BlockSpec((bm, k), lambda i, j: (i, 0)),
                pl.BlockSpec((None, k, bn), lambda i, j: (l, 0, j))]
    out_spec = pl.BlockSpec((bm, bn), lambda i, j: (i, j))
    out_shape = jax.ShapeDtypeStruct((m, n), out_dtype)
    if cast is None:
        return pl.pallas_call(
            _mm_kernel,
            grid=grid,
            in_specs=in_specs,
            out_specs=out_spec,
            out_shape=out_shape,
            compiler_params=_params("parallel", "parallel"),
            name="matmul",
        )(x, w)
    table, layer = cast
    assert table.shape[1:] == w.shape[1:] and k % grid[0] == 0
    rows = k // grid[0]
    assert rows % BF16_SUBLANES == 0
    return pl.pallas_call(
        _mm_cast_kernel,
        grid=grid,
        in_specs=in_specs + [pl.BlockSpec((None, rows, bn), lambda i, j: (layer, i, j))],
        out_specs=[out_spec, pl.BlockSpec((None, rows, bn), lambda i, j: (0, i, j))],
        out_shape=[out_shape, jax.ShapeDtypeStruct((1, k, n), _BF16)],
        compiler_params=_params("parallel", "parallel"),
        name="matmul_cast",
    )(x, w, table)


def _mm_retrieve_kernel(x_ref, w_ref, sc_ref, *refs, n_casts):
    srcs, (o_ref, i1_ref, i2_ref, g_ref), dsts = (
        refs[:n_casts], refs[n_casts:n_casts + 4], refs[n_casts + 4:])
    for src_ref, dst_ref in zip(srcs, dsts):
        dst_ref[...] = src_ref[...].astype(dst_ref.dtype)
    cw = min(sc_ref.shape[2], RETRIEVE_CHUNK_LANES)
    parts = sc_ref.shape[2] // cw
    rows = x_ref.shape[0] // parts

    def part(c, carry):
        lanes = pl.ds(pl.multiple_of(c * cw, cw), cw)
        i1, i2, g = _retrieve_chunk(sc_ref[0, :, lanes], sc_ref[1, :, lanes])
        i1_ref[:, lanes] = i1
        i2_ref[:, lanes] = i2
        g_ref[:, lanes] = g
        r = pl.ds(pl.multiple_of(c * rows, rows), rows)
        o_ref[r, :] = jnp.dot(x_ref[r, :], w_ref[...],
                              preferred_element_type=_F32).astype(o_ref.dtype)
        return carry

    lax.fori_loop(0, parts, part, 0)


def _mm_hosting(x, w, l, out_dtype, scores, casts=(), bm=512, bn=2304):
    m, k = x.shape
    n = w.shape[2]
    bm, bn = _blk(m, bm), _blk(n, bn)
    steps_m = m // bm
    steps = steps_m * (n // bn)
    hs, keys_n, tg = scores.shape
    heads = hs // 2
    topk = PEER_TOPK
    assert (tg * heads) % steps == 0
    tmg = tg * heads // steps
    assert tmg % LANES == 0 and tg % tmg == 0
    tiles = tg // tmg

    def head(j, i):
        return (j * steps_m + i) // tiles

    def tile(j, i):
        return (j * steps_m + i) % tiles

    r_shape = jax.ShapeDtypeStruct((heads * topk, tg), _F32)
    r_spec = pl.BlockSpec((topk, tmg), lambda j, i: (head(j, i), tile(j, i)))
    cast_in, cast_out, cast_shape = [], [], []
    for table, layer in casts:
        rows, cols = table.shape[1:]
        assert rows % steps == 0 and (rows // steps) % BF16_SUBLANES == 0
        slab = rows // steps
        cast_in.append(pl.BlockSpec((None, slab, cols),
                                    lambda j, i, layer=layer: (layer, j * steps_m + i, 0)))
        cast_out.append(pl.BlockSpec((slab, cols), lambda j, i: (j * steps_m + i, 0)))
        cast_shape.append(jax.ShapeDtypeStruct((rows, cols), _BF16))
    w_mode = dict(pipeline_mode=pl.Buffered(1)) if casts else {}
    out, i1, i2, g, *copies = pl.pallas_call(
        functools.partial(_mm_retrieve_kernel, n_casts=len(casts)),
        grid=(n // bn, steps_m),
        in_specs=[pl.BlockSpec((bm, k), lambda j, i: (i, 0)),
                  pl.BlockSpec((None, k, bn), lambda j, i: (l, 0, j), **w_mode),
                  pl.BlockSpec((2, keys_n, tmg), lambda j, i: (head(j, i), 0, tile(j, i)))]
        + cast_in,
        out_specs=[pl.BlockSpec((bm, bn), lambda j, i: (i, j)), r_spec, r_spec, r_spec]
        + cast_out,
        out_shape=[jax.ShapeDtypeStruct((m, n), out_dtype), r_shape, r_shape, r_shape]
        + cast_shape,
        compiler_params=_params("parallel", "parallel"),
        name="matmul_retrieve",
    )(x, w, scores, *[table for table, _ in casts])
    return out, (i1, i2, g), copies


def _mm_res_kernel(x_ref, w_ref, r_ref, o_ref):
    o_ref[...] = r_ref[...] + jnp.dot(x_ref[...], w_ref[...], preferred_element_type=_F32)


def _mm_res(x, w, l, res, bm=1024, bn=1024):
    m, k = x.shape
    n = w.shape[2]
    bm, bn = _blk(m, bm), _blk(n, bn)
    return pl.pallas_call(
        _mm_res_kernel,
        grid=(m // bm, n // bn),
        in_specs=[pl.BlockSpec((bm, k), lambda i, j: (i, 0)),
                  pl.BlockSpec((None, k, bn), lambda i, j: (l, 0, j)),
                  pl.BlockSpec((bm, bn), lambda i, j: (i, j))],
        out_specs=pl.BlockSpec((bm, bn), lambda i, j: (i, j)),
        out_shape=jax.ShapeDtypeStruct((m, n), _F32),
        compiler_params=_params("parallel", "parallel"),
        name="matmul_residual",
    )(x, w, res)


def _merge_kernel(ya_ref, yb_ref, pa_ref, pb_ref, ga_ref, gb_ref, o_ref):
    a = jnp.dot(ya_ref[...], pa_ref[...], preferred_element_type=_F32)
    b = jnp.dot(yb_ref[...], pb_ref[...], preferred_element_type=_F32)
    ga = jax.nn.sigmoid(ga_ref[...].astype(_F32))
    gb = jax.nn.sigmoid(gb_ref[...].astype(_F32))
    o_ref[...] = (ga * a + gb * b).astype(o_ref.dtype)


def _merge(ya, yb, pa, pb, l, z, gate_col0, bm=1024, bn=1024):
    m, k = ya.shape
    n = pa.shape[2]
    bm, bn = _blk(m, bm), _blk(math.gcd(n, gate_col0), bn)
    ga0 = gate_col0 // bn
    gb0 = (gate_col0 + n) // bn
    return pl.pallas_call(
        _merge_kernel,
        grid=(m // bm, n // bn),
        in_specs=[pl.BlockSpec((bm, k), lambda i, j: (i, 0)),
                  pl.BlockSpec((bm, k), lambda i, j: (i, 0)),
                  pl.BlockSpec((None, k, bn), lambda i, j: (l, 0, j)),
                  pl.BlockSpec((None, k, bn), lambda i, j: (l, 0, j)),
                  pl.BlockSpec((bm, bn), lambda i, j: (i, ga0 + j)),
                  pl.BlockSpec((bm, bn), lambda i, j: (i, gb0 + j))],
        out_specs=pl.BlockSpec((bm, bn), lambda i, j: (i, j)),
        out_shape=jax.ShapeDtypeStruct((m, n), _BF16),
        compiler_params=_params("parallel", "parallel"),
        name="merge_matmul",
    )(ya, yb, pa, pb, z, z)


def _mixer_kernel(xin_ref, gb_ref, gc_ref, u_ref, v_ref,
                  xin_p_ref, gc_p_ref, xin_n_ref, gc_n_ref,
                  cw_ref, lng_ref, lnb_ref, sw_ref, sb_ref,
                  ya_ref, yb_ref, gv_ref, *, seq_len, chunk, groups):
    i = pl.program_id(0)
    tm, width = xin_ref.shape
    gd = width // groups
    row0 = i * tm
    has_prev = (row0 % seq_len != 0).astype(_F32)
    has_next = ((row0 + tm) % seq_len != 0).astype(_F32)
    rows = lax.broadcasted_iota(jnp.int32, (tm, gd), 0)

    def lane_fold(x):
        return functools.reduce(
            lambda p, q: p + q, [x[:, k * LANES:(k + 1) * LANES] for k in range(gd // LANES)])

    row_sum = jnp.zeros((tm, LANES), _F32)
    for g in range(groups):
        cols = slice(g * gd, (g + 1) * gd)
        zc = gc_ref[:, cols].astype(_F32) * xin_ref[:, cols].astype(_F32)
        zp = gc_p_ref[:, cols].astype(_F32) * xin_p_ref[:, cols].astype(_F32)
        zn = gc_n_ref[:, cols].astype(_F32) * xin_n_ref[:, cols].astype(_F32)
        halo_p = zp[BF16_SUBLANES - 1:BF16_SUBLANES, :] * has_prev
        halo_n = zn[0:1, :] * has_next
        prev = jnp.where(rows == 0, halo_p, pltpu.roll(zc, 1, 0))
        nxt = jnp.where(rows == tm - 1, halo_n, pltpu.roll(zc, tm - 1, 0))
        cw = cw_ref[:, cols]
        conv = prev * cw[0:1, :] + zc * cw[1:2, :] + nxt * cw[2:3, :]
        ya_ref[:, cols] = (gb_ref[:, cols].astype(_F32) * conv).astype(ya_ref.dtype)
        gv = _gelu(v_ref[:, cols].astype(_F32))
        gv_ref[:, cols] = gv
        row_sum = row_sum + lane_fold(gv)

    mu = jnp.sum(row_sum, axis=-1, keepdims=True) * (1.0 / width)
    sq_sum = jnp.zeros((tm, LANES), _F32)
    for g in range(groups):
        vc = gv_ref[:, g * gd:(g + 1) * gd] - mu
        sq_sum = sq_sum + lane_fold(vc * vc)
    rstd = lax.rsqrt(jnp.sum(sq_sum, axis=-1, keepdims=True) * (1.0 / width) + EPS)
    for g in range(groups):
        cols = slice(g * gd, (g + 1) * gd)
        vn = ((gv_ref[:, cols] - mu) * rstd * lng_ref[:, cols] + lnb_ref[:, cols]).astype(_BF16)
        for c in range(tm // chunk):
            r = slice(c * chunk, (c + 1) * chunk)
            mixed = jnp.dot(sw_ref[g], vn[r, :], preferred_element_type=_F32) + sb_ref[:, cols]
            u = _gelu(u_ref[r, cols].astype(_F32))
            yb_ref[r, cols] = (u * mixed).astype(yb_ref.dtype)


def _mixer(z, conv_w, ln_g, ln_b, sg_w, sg_b, seq_len):
    t = z.shape[0]
    width = conv_w.shape[1]
    groups, chunk, _ = sg_w.shape
    assert ln_g.shape[0] == width, "conv and spatial-gating widths must match"
    tm = _blk(seq_len, 2 * chunk)
    assert tm % chunk == 0 and tm % BF16_SUBLANES == 0
    hb = tm // BF16_SUBLANES
    last_hb = t // BF16_SUBLANES - 1
    gd = width // groups
    sb_full = jnp.repeat(sg_b.T, gd, axis=1)

    def col(c):
        return pl.BlockSpec((tm, width), lambda i: (i, c))

    def halo_prev(c):
        return pl.BlockSpec((BF16_SUBLANES, width),
                            lambda i: (jnp.maximum(i * hb - 1, 0), c))

    def halo_next(c):
        return pl.BlockSpec((BF16_SUBLANES, width),
                            lambda i: (jnp.minimum((i + 1) * hb, last_hb), c))

    def full(shape):
        return pl.BlockSpec(shape, lambda i: (0,) * len(shape))

    kern = functools.partial(_mixer_kernel, seq_len=seq_len, chunk=chunk, groups=groups)
    return pl.pallas_call(
        kern,
        grid=(t // tm,),
        in_specs=[col(0), col(1), col(2), col(3), col(4),
                  halo_prev(0), halo_prev(2), halo_next(0), halo_next(2),
                  full((3, width)), full((1, width)), full((1, width)),
                  full((groups, chunk, chunk)), full((chunk, width))],
        out_specs=[pl.BlockSpec((tm, width), lambda i: (i, 0)),
                   pl.BlockSpec((tm, width), lambda i: (i, 0))],
        out_shape=[jax.ShapeDtypeStruct((t, width), _BF16),
                   jax.ShapeDtypeStruct((t, width), _BF16)],
        scratch_shapes=[pltpu.VMEM((tm, width), _F32)],
        compiler_params=_params("parallel"),
        name="mixer",
    )(z, z, z, z, z, z, z, z, z,
      conv_w, ln_g.reshape(1, width), ln_b.reshape(1, width),
      sg_w.astype(_BF16), sb_full)


def _top_rows(sc, k):
    n = sc.shape[0]
    rows = lax.broadcasted_iota(jnp.int32, sc.shape, 0)
    vals, idxs = [], []
    for _ in range(k):
        m = jnp.max(sc, axis=0, keepdims=True)
        idx = jnp.min(jnp.where(sc == m, rows, n), axis=0, keepdims=True)
        sc = jnp.where(rows == idx, -jnp.inf, sc)
        vals.append(m)
        idxs.append(idx)
    return vals, idxs


def _candidate_groups(k):
    return [(k1, k // (k1 + 1)) for k1 in range(k)]


def _query_scores_kernel(x_ref, g_ref, wq_ref, keys_ref, h_ref, sc_ref):
    @pl.when(pl.program_id(1) == 0)
    def _():
        rows = min(x_ref.shape[0], NORM_CHUNK_ROWS)

        def chunk(c, carry):
            r = pl.ds(pl.multiple_of(c * rows, rows), rows)
            _rmsnorm_kernel(x_ref.at[r, :], g_ref, h_ref.at[r, :])
            return carry

        lax.fori_loop(0, x_ref.shape[0] // rows, chunk, 0)

    hk = keys_ref.shape[2]
    q = jnp.dot(h_ref[...], wq_ref[...], preferred_element_type=_F32)
    q_hi, q_lo = _split_bf16(q)
    nt = (((1,), (1,)), ((), ()))
    for hs in range(keys_ref.shape[0]):
        k_hi, k_lo = _split_bf16(keys_ref[hs])
        cols = slice(hs * hk, (hs + 1) * hk)
        sc_ref[hs] = (lax.dot_general(k_hi, q_hi[:, cols], nt, preferred_element_type=_F32)
                      + lax.dot_general(k_hi, q_lo[:, cols], nt, preferred_element_type=_F32)
                      + lax.dot_general(k_lo, q_hi[:, cols], nt, preferred_element_type=_F32))


def _query_scores(x, g, wq, l, keys, bm=512, bn=1024):
    t, d = x.shape
    heads, _, n_keys, hk = keys.shape
    n = heads * 2 * hk
    bm, bn = _blk(t, bm), _blk(n, bn)
    assert bn % hk == 0
    per = bn // hk
    return pl.pallas_call(
        _query_scores_kernel,
        grid=(t // bm, n // bn),
        in_specs=[pl.BlockSpec((bm, d), lambda i, j: (i, 0)),
                  pl.BlockSpec((1, d), lambda i, j: (0, 0)),
                  pl.BlockSpec((None, d, bn), lambda i, j: (l, 0, j)),
                  pl.BlockSpec((per, n_keys, hk), lambda i, j: (j, 0, 0))],
        out_specs=[pl.BlockSpec((bm, d), lambda i, j: (i, 0)),
                   pl.BlockSpec((per, n_keys, bm), lambda i, j: (j, 0, i))],
        out_shape=[jax.ShapeDtypeStruct((t, d), _BF16),
                   jax.ShapeDtypeStruct((heads * 2, n_keys, t), _F32)],
        compiler_params=_params("parallel", "arbitrary"),
        name="query_scores",
    )(x, g.reshape(1, d), wq, keys.reshape(heads * 2, n_keys, hk))


def _retrieve_kernel(sc_ref, i1_ref, i2_ref, g_ref):
    tm = sc_ref.shape[2]
    cw = min(tm, RETRIEVE_CHUNK_LANES)
    for c in range(tm // cw):
        cols = slice(c * cw, (c + 1) * cw)
        i1, i2, g = _retrieve_chunk(sc_ref[0, :, cols], sc_ref[1, :, cols])
        i1_ref[:, cols] = i1
        i2_ref[:, cols] = i2
        g_ref[:, cols] = g


def _retrieve_chunk(sc1, sc2):
    k = PEER_TOPK
    (s1, i1), (s2, i2) = _top_rows(sc1, k), _top_rows(sc2, k)
    cand, c1, c2 = [], [], []
    for k1, n2 in _candidate_groups(k):
        for k2 in range(n2):
            cand.append(s1[k1] + s2[k2])
            c1.append(i1[k1])
            c2.append(i2[k2])
    n_c = len(cand)
    pad = (-n_c) % 8
    lanes = cand[0].shape[1]
    cand = jnp.concatenate(cand + [jnp.full((pad, lanes), -jnp.inf, _F32)], axis=0)
    c1 = jnp.concatenate(c1 + [jnp.zeros((pad, lanes), jnp.int32)], axis=0)
    c2 = jnp.concatenate(c2 + [jnp.zeros((pad, lanes), jnp.int32)], axis=0)
    rows = lax.broadcasted_iota(jnp.int32, cand.shape, 0)
    top, e1, e2 = [], [], []
    for _ in range(k):
        m = jnp.max(cand, axis=0, keepdims=True)
        idx = jnp.min(jnp.where(cand == m, rows, n_c + pad), axis=0, keepdims=True)
        sel = rows == idx
        e1.append(jnp.sum(jnp.where(sel, c1, 0), axis=0, keepdims=True))
        e2.append(jnp.sum(jnp.where(sel, c2, 0), axis=0, keepdims=True))
        cand = jnp.where(sel, -jnp.inf, cand)
        top.append(m)
    top = jnp.concatenate(top, axis=0)
    ex = jnp.exp(top - top[0:1, :])
    gates = ex / jnp.sum(ex, axis=0, keepdims=True)
    return (jnp.concatenate(e1, axis=0).astype(_F32),
            jnp.concatenate(e2, axis=0).astype(_F32), gates)


def _gate_matrix_kernel(i1_ref, i2_ref, g_ref, w_ref, s_ref, *, n_keys):
    tm, slots = i1_ref.shape
    i1 = i1_ref[...]
    i2 = i2_ref[...]
    gt = g_ref[...]
    key_ids = lax.broadcasted_iota(jnp.int32, (n_keys, slots), 0).astype(_F32).astype(_BF16)
    one = jnp.ones((n_keys, slots), _BF16)
    zero = jnp.zeros((n_keys, slots), _BF16)
    grp = BF16_SUBLANES
    n_grp = tm // grp

    ext = n_keys + 2 * BF16_SUBLANES
    ext_ids = (lax.broadcasted_iota(jnp.int32, (ext, slots), 0)
               - W_PITCH_SHIFT).astype(_F32).astype(_BF16)
    ext_one = jnp.ones((ext, slots), _BF16)
    ext_zero = jnp.zeros((ext, slots), _BF16)
    dims = (((1,), (1,)), ((), ()))

    def per_token_products(g):
        for tl in range(grp):
            t = g * grp + tl
            r2 = jnp.broadcast_to(i2[t:t + 1, :].astype(_BF16), (n_keys, slots))
            rg = jnp.broadcast_to(gt[t:t + 1, :].astype(_BF16), (n_keys, slots))
            b_val = jnp.where(key_ids == r2, rg, zero)
            row0 = tl * W_PITCH
            if tl % 2 == 0:
                r1 = jnp.broadcast_to(i1[t:t + 1, :].astype(_BF16), (n_keys, slots))
                a_hot = jnp.where(key_ids == r1, one, zero)
                s_ref[g % 2, row0:row0 + n_keys, :] = lax.dot_general(
                    a_hot, b_val, dims, preferred_element_type=_F32)
            else:
                r1 = jnp.broadcast_to(i1[t:t + 1, :].astype(_BF16), (ext, slots))
                a_hot = jnp.where(ext_ids == r1, ext_one, ext_zero)
                prod = lax.dot_general(a_hot, b_val, dims, preferred_element_type=_F32)
                lo = row0 - W_PITCH_SHIFT
                s_ref[g % 2, lo:lo + n_keys + 8, :] = prod[:n_keys + 8, :]

    def gather_rows(g):
        for a in range(n_keys):
            w_ref[g * grp:(g + 1) * grp, a * n_keys:(a + 1) * n_keys] = (
                s_ref[g % 2, pl.ds(a, grp, stride=W_PITCH), :].astype(w_ref.dtype))

    for g in range(n_grp + 1):
        if g < n_grp:
            per_token_products(g)
        if g > 0:
            gather_rows(g - 1)


def _gate_matrix(i1, i2, gates, n_keys):
    t, slots = i1.shape
    assert n_keys == LANES
    tm = _blk(t, LANES)
    spec = pl.BlockSpec((tm, slots), lambda i: (i, 0))
    return pl.pallas_call(
        functools.partial(_gate_matrix_kernel, n_keys=n_keys),
        grid=(t // tm,),
        in_specs=[spec, spec, spec],
        out_specs=pl.BlockSpec((tm, n_keys * n_keys), lambda i: (i, 0)),
        out_shape=jax.ShapeDtypeStruct((t, n_keys * n_keys), _BF16),
        scratch_shapes=[pltpu.VMEM((2, BF16_SUBLANES * W_PITCH, LANES), _F32)],
        compiler_params=_params("parallel"),
        name="peer_gate_matrix",
    )(i1, i2, gates)


def _experts_kernel(h_ref, u_ref, v_ref, w_ref, x_hbm, o_ref, x_sem):
    i, n = pl.program_id(0), pl.program_id(1)
    bm = o_ref.shape[0]

    def residual_copy():
        return pltpu.make_async_copy(
            x_hbm.at[pl.ds(pl.multiple_of(i * bm, bm), bm), :], o_ref, x_sem)

    @pl.when(n == 0)
    def _():
        residual_copy().start()

    bn = u_ref.shape[0]
    sub = min(bn, MXU_COLUMNS)

    parts = bn // sub
    pres = [lax.dot_general(h_ref[...], u_ref[c * sub:(c + 1) * sub, :],
                            (((1,), (1,)), ((), ())), preferred_element_type=_F32)
            for c in range(parts)]

    def gated(c):
        return (w_ref[:, c * sub:(c + 1) * sub].astype(_F32) * _gelu(pres[c])).astype(_BF16)

    act = gated(0)

    @pl.when(n == 0)
    def _():
        residual_copy().wait()

    for c in range(parts):
        o_ref[...] += jnp.dot(act, v_ref[c * sub:(c + 1) * sub, :],
                              preferred_element_type=_F32)
        if c + 1 < parts:
            act = gated(c + 1)


def _experts_retrieve_kernel(h_ref, u_ref, v_ref, w_ref, x_hbm, sc_ref,
                             o_ref, i1_ref, i2_ref, g_ref, x_sem):
    _retrieve_kernel(sc_ref, i1_ref, i2_ref, g_ref)
    _experts_kernel(h_ref, u_ref, v_ref, w_ref, x_hbm, o_ref, x_sem)


def _experts_retrieve_cast_kernel(h_ref, u_ref, v_ref, w_ref, x_hbm, sc_ref, src_ref,
                                  o_ref, i1_ref, i2_ref, g_ref, dst_ref, x_sem):
    dst_ref[...] = src_ref[...].astype(dst_ref.dtype)
    _experts_retrieve_kernel(h_ref, u_ref, v_ref, w_ref, x_hbm, sc_ref,
                             o_ref, i1_ref, i2_ref, g_ref, x_sem)


def _experts(x, h, u, v, w, guest=None, cast=None, bm=1024, bn=512):
    t, d = h.shape
    e = u.shape[0]
    bm, bn = _blk(t, bm), _blk(e, bn)
    steps_n = e // bn
    steps = (t // bm) * steps_n
    in_specs = [pl.BlockSpec((bm, d), lambda i, n: (i, 0)),
                pl.BlockSpec((bn, d), lambda i, n: (n, 0)),
                pl.BlockSpec((bn, d), lambda i, n: (n, 0)),
                pl.BlockSpec((bm, bn), lambda i, n: (i, n)),
                pl.BlockSpec(memory_space=pl.ANY)]
    out_spec = pl.BlockSpec((bm, d), lambda i, n: (i, 0), pipeline_mode=pl.Buffered(1))
    out_shape = jax.ShapeDtypeStruct((t, d), _F32)
    scratch = [pltpu.SemaphoreType.DMA(())]
    if guest is None:
        assert cast is None
        out = pl.pallas_call(
            _experts_kernel,
            grid=(t // bm, steps_n),
            in_specs=in_specs,
            out_specs=out_spec,
            out_shape=out_shape,
            scratch_shapes=scratch,
            compiler_params=_params("parallel", "arbitrary"),
            name="peer_experts",
        )(h, u, v, w, x)
        return out, None, None

    hs, n_keys, tg = guest.shape
    heads = hs // 2
    k = PEER_TOPK
    assert (tg * heads) % steps == 0
    tmg = tg * heads // steps
    assert tmg % LANES == 0 and tg % tmg == 0

    def tile(i, n):
        return (i * steps_n + n) // heads

    def head(i, n):
        return (i * steps_n + n) % heads

    r_shape = jax.ShapeDtypeStruct((heads * k, tg), _F32)
    r_spec = pl.BlockSpec((k, tmg), lambda i, n: (head(i, n), tile(i, n)))
    in_specs = in_specs + [
        pl.BlockSpec((2, n_keys, tmg), lambda i, n: (head(i, n), 0, tile(i, n)))]
    out_specs = [out_spec, r_spec, r_spec, r_spec]
    out_shapes = [out_shape, r_shape, r_shape, r_shape]
    if cast is None:
        out, i1, i2, g = pl.pallas_call(
            _experts_retrieve_kernel,
            grid=(t // bm, steps_n),
            in_specs=in_specs,
            out_specs=out_specs,
            out_shape=out_shapes,
            scratch_shapes=scratch,
            compiler_params=_params("parallel", "arbitrary"),
            name="peer_experts_retrieve",
        )(h, u, v, w, x, guest)
        return out, (i1, i2, g), None

    table, layer = cast
    rows, cols = table.shape[1:]
    assert rows % steps == 0 and (rows // steps) % BF16_SUBLANES == 0
    slab = rows // steps
    out, i1, i2, g, copy = pl.pallas_call(
        _experts_retrieve_cast_kernel,
        grid=(t // bm, steps_n),
        in_specs=in_specs + [
            pl.BlockSpec((None, slab, cols), lambda i, n: (layer, i * steps_n + n, 0))],
        out_specs=out_specs + [pl.BlockSpec((slab, cols), lambda i, n: (i * steps_n + n, 0))],
        out_shape=out_shapes + [jax.ShapeDtypeStruct((rows, cols), _BF16)],
        scratch_shapes=scratch,
        compiler_params=_params("parallel", "arbitrary"),
        name="peer_experts_retrieve_cast",
    )(h, u, v, w, x, guest, table)
    return out, (i1, i2, g), copy


def _mixer_block(x, h, p, l, seq_len, w_in, scores=None, cast=None, casts=()):
    width = p["conv_w"].shape[-1]
    if scores is not None:
        z, *guest = _mm_hosting(h, *w_in, _BF16, scores, casts)
    elif cast is not None:
        z, guest = _mm(h, *w_in, _BF16, cast=cast)
    else:
        z, guest = _mm(h, *w_in, _BF16), None
    ya, yb = _mixer(z, p["conv_w"][l], p["sg_ln_g"][l], p["sg_ln_b"][l],
                    p["sg_w"][l], p["sg_b"][l], seq_len)
    merged = _merge(ya, yb, p["proj_a"], p["proj_b"], l, z, 5 * width)
    x = _mm_res(merged, p["w_o"], l, x)
    h, sc = _query_scores(x, p["norm2_g"][l], p["peer_wq"], l, p["peer_keys"][l])
    return x, h, sc, guest


def _slots_last(picks):
    return tuple(a.T for a in picks)


def _trunk_pair(xa3, xb3, p):
    depth = p["w_in"].shape[0]
    n_keys = p["peer_keys"].shape[-2]
    d = xa3.shape[-1]
    seq_a, seq_b = xa3.shape[1], xb3.shape[1]
    xa, xb = xa3.reshape(-1, d), xb3.reshape(-1, d)
    ha = _rmsnorm(xa, p["norm1_g"][0], _BF16)
    hb = _rmsnorm(xb, p["norm1_g"][0], _BF16)
    w_in = (p["w_in"][0:1].astype(_BF16), 0)

    def next_w_in(l):
        return (p["w_in"], l + 1) if l + 1 < depth else None

    xa, ha, sa, w_next = _mixer_block(xa, ha, p, 0, seq_a, w_in, cast=next_w_in(0))
    xb, hb, sb, (ra, (u, v)) = _mixer_block(xb, hb, p, 0, seq_b, w_in, scores=sa,
                                            casts=((p["peer_u"], 0), (p["peer_v"], 0)))
    wa = _gate_matrix(*_slots_last(ra), n_keys)
    for l in range(depth):
        last = l + 1 == depth
        gain, out_dtype = (p["final_g"], _F32) if last else (p["norm1_g"][l + 1], _BF16)
        xa, rb, u_next = _experts(xa, ha, u, v, wa, guest=sb,
                                  cast=None if last else (p["peer_u"], l + 1))
        ha = _rmsnorm(xa, gain, out_dtype)
        wb = _gate_matrix(*_slots_last(rb), n_keys)
        if last:
            xb, _, _ = _experts(xb, hb, u, v, wb)
            hb = _rmsnorm(xb, gain, out_dtype)
        else:
            w_in = (w_next, 0)
            xa, ha, sa, w_next = _mixer_block(xa, ha, p, l + 1, seq_a, w_in,
                                              cast=next_w_in(l + 1))
            xb, ra, v_next = _experts(xb, hb, u, v, wb, guest=sa, cast=(p["peer_v"], l + 1))
            hb = _rmsnorm(xb, gain, out_dtype)
            xb, hb, sb, _ = _mixer_block(xb, hb, p, l + 1, seq_b, w_in)
            wa = _gate_matrix(*_slots_last(ra), n_keys)
            u, v = u_next, v_next
    return ha.reshape(xa3.shape), hb.reshape(xb3.shape)


def kernel(x_prompt, x_sample, norm1_g, w_in, conv_w, sg_ln_g, sg_ln_b, sg_w, sg_b, proj_a, proj_b, w_o, norm2_g, peer_wq, peer_keys, peer_u, peer_v, final_g):
    p = dict(
        norm1_g=norm1_g, conv_w=conv_w, sg_ln_g=sg_ln_g, sg_ln_b=sg_ln_b, sg_w=sg_w, sg_b=sg_b,
        norm2_g=norm2_g, peer_keys=peer_keys, final_g=final_g, peer_u=peer_u, peer_v=peer_v,
        w_in=w_in,
        proj_a=proj_a.astype(_BF16), proj_b=proj_b.astype(_BF16),
        w_o=w_o.astype(_BF16), peer_wq=peer_wq.astype(_BF16),
    )
    return _trunk_pair(x_prompt, x_sample, p)
```

```python
import functools
import math

import jax
import jax.numpy as jnp
from jax import lax
from jax.experimental import pallas as pl
from jax.experimental.pallas import tpu as pltpu

PEER_TOPK = 16
EPS = 1e-6
V7X_VMEM_LIMIT_BYTES = 56 * 1024 * 1024
BF16_SUBLANES = 16
LANES = 128
MXU_COLUMNS = 256
W_PITCH = 132
W_PITCH_SHIFT = W_PITCH % 8
RETRIEVE_CHUNK_LANES = 256
NORM_CHUNK_ROWS = 64

_F32 = jnp.float32
_BF16 = jnp.bfloat16


def _params(*semantics):
    return pltpu.CompilerParams(dimension_semantics=semantics,
                                vmem_limit_bytes=V7X_VMEM_LIMIT_BYTES)


def _blk(n, target):
    if n <= target:
        return n
    for b in range(target - target % LANES, 0, -LANES):
        if n % b == 0:
            return b
    raise ValueError(f"no lane-aligned block for {n} <= {target}")


def _split_bf16(x):
    hi = x.astype(_BF16)
    return hi, (x - hi.astype(_F32)).astype(_BF16)


def _gelu(x):
    return 0.5 * x * (1.0 + lax.erf(x * (1.0 / math.sqrt(2.0))))


def _rmsnorm_kernel(x_ref, g_ref, h_ref):
    x = x_ref[...]
    ms = jnp.mean(x * x, axis=-1, keepdims=True)
    h_ref[...] = (x * lax.rsqrt(ms + EPS) * g_ref[...]).astype(h_ref.dtype)


def _rmsnorm(x, g, out_dtype):
    t, d = x.shape
    bm = _blk(t, 512)
    return pl.pallas_call(
        _rmsnorm_kernel,
        grid=(t // bm,),
        in_specs=[pl.BlockSpec((bm, d), lambda i: (i, 0)),
                  pl.BlockSpec((1, d), lambda i: (0, 0))],
        out_specs=pl.BlockSpec((bm, d), lambda i: (i, 0)),
        out_shape=jax.ShapeDtypeStruct((t, d), out_dtype),
        compiler_params=_params("parallel"),
        name="rmsnorm",
    )(x, g.reshape(1, d))


def _mm_kernel(x_ref, w_ref, o_ref):
    o_ref[...] = jnp.dot(x_ref[...], w_ref[...],
                         preferred_element_type=_F32).astype(o_ref.dtype)


def _mm_cast_kernel(x_ref, w_ref, src_ref, o_ref, dst_ref):
    dst_ref[...] = src_ref[...].astype(dst_ref.dtype)
    _mm_kernel(x_ref, w_ref, o_ref)


def _mm(x, w, l, out_dtype, cast=None, bm=1024, bn=1024):
    m, k = x.shape
    n = w.shape[2]
    bm, bn = _blk(m, bm), _blk(n, bn)
    grid = (m // bm, n // bn)
    in_specs = [pl.BlockSpec((bm, k), lambda i, j: (i, 0)),
                pl.BlockSpec((None, k, bn), lambda i, j: (l, 0, j))]
    out_spec = pl.BlockSpec((bm, bn), lambda i, j: (i, j))
    out_shape = jax.ShapeDtypeStruct((m, n), out_dtype)
    if cast is None:
        return pl.pallas_call(
            _mm_kernel,
            grid=grid,
            in_specs=in_specs,
            out_specs=out_spec,
            out_shape=out_shape,
            compiler_params=_params("parallel", "parallel"),
            name="matmul",
        )(x, w)
    table, layer = cast
    assert table.shape[1:] == w.shape[1:] and k % grid[0] == 0
    rows = k // grid[0]
    assert rows % BF16_SUBLANES == 0
    return pl.pallas_call(
        _mm_cast_kernel,
        grid=grid,
        in_specs=in_specs + [pl.BlockSpec((None, rows, bn), lambda i, j: (layer, i, j))],
        out_specs=[out_spec, pl.BlockSpec((None, rows, bn), lambda i, j: (0, i, j))],
        out_shape=[out_shape, jax.ShapeDtypeStruct((1, k, n), _BF16)],
        compiler_params=_params("parallel", "parallel"),
        name="matmul_cast",
    )(x, w, table)


def _mm_retrieve_kernel(x_ref, w_ref, sc_ref, *refs, n_casts):
    srcs, (o_ref, i1_ref, i2_ref, g_ref), dsts = (
        refs[:n_casts], refs[n_casts:n_casts + 4], refs[n_casts + 4:])
    for src_ref, dst_ref in zip(srcs, dsts):
        dst_ref[...] = src_ref[...].astype(dst_ref.dtype)
    cw = min(sc_ref.shape[2], RETRIEVE_CHUNK_LANES)
    parts = sc_ref.shape[2] // cw
    rows = x_ref.shape[0] // parts

    def part(c, carry):
        lanes = pl.ds(pl.multiple_of(c * cw, cw), cw)
        i1, i2, g = _retrieve_chunk(sc_ref[0, :, lanes], sc_ref[1, :, lanes])
        i1_ref[:, lanes] = i1
        i2_ref[:, lanes] = i2
        g_ref[:, lanes] = g
        r = pl.ds(pl.multiple_of(c * rows, rows), rows)
        o_ref[r, :] = jnp.dot(x_ref[r, :], w_ref[...],
                              preferred_element_type=_F32).astype(o_ref.dtype)
        return carry

    lax.fori_loop(0, parts, part, 0)


def _mm_hosting(x, w, l, out_dtype, scores, casts=(), bm=512, bn=2304):
    m, k = x.shape
    n = w.shape[2]
    bm, bn = _blk(m, bm), _blk(n, bn)
    steps_m = m // bm
    steps = steps_m * (n // bn)
    hs, keys_n, tg = scores.shape
    heads = hs // 2
    topk = PEER_TOPK
    assert (tg * heads) % steps == 0
    tmg = tg * heads // steps
    assert tmg % LANES == 0 and tg % tmg == 0
    tiles = tg // tmg

    def head(j, i):
        return (j * steps_m + i) // tiles

    def tile(j, i):
        return (j * steps_m + i) % tiles

    r_shape = jax.ShapeDtypeStruct((heads * topk, tg), _F32)
    r_spec = pl.BlockSpec((topk, tmg), lambda j, i: (head(j, i), tile(j, i)))
    cast_in, cast_out, cast_shape = [], [], []
    for table, layer in casts:
        rows, cols = table.shape[1:]
        assert rows % steps == 0 and (rows // steps) % BF16_SUBLANES == 0
        slab = rows // steps
        cast_in.append(pl.BlockSpec((None, slab, cols),
                                    lambda j, i, layer=layer: (layer, j * steps_m + i, 0)))
        cast_out.append(pl.BlockSpec((slab, cols), lambda j, i: (j * steps_m + i, 0)))
        cast_shape.append(jax.ShapeDtypeStruct((rows, cols), _BF16))
    w_mode = dict(pipeline_mode=pl.Buffered(1)) if casts else {}
    out, i1, i2, g, *copies = pl.pallas_call(
        functools.partial(_mm_retrieve_kernel, n_casts=len(casts)),
        grid=(n // bn, steps_m),
        in_specs=[pl.BlockSpec((bm, k), lambda j, i: (i, 0)),
                  pl.BlockSpec((None, k, bn), lambda j, i: (l, 0, j), **w_mode),
                  pl.BlockSpec((2, keys_n, tmg), lambda j, i: (head(j, i), 0, tile(j, i)))]
        + cast_in,
        out_specs=[pl.BlockSpec((bm, bn), lambda j, i: (i, j)), r_spec, r_spec, r_spec]
        + cast_out,
        out_shape=[jax.ShapeDtypeStruct((m, n), out_dtype), r_shape, r_shape, r_shape]
        + cast_shape,
        compiler_params=_params("parallel", "parallel"),
        name="matmul_retrieve",
    )(x, w, scores, *[table for table, _ in casts])
    return out, (i1, i2, g), copies


def _mm_res_kernel(x_ref, w_ref, r_ref, o_ref):
    o_ref[...] = r_ref[...] + jnp.dot(x_ref[...], w_ref[...], preferred_element_type=_F32)


def _mm_res(x, w, l, res, bm=1024, bn=1024):
    m, k = x.shape
    n = w.shape[2]
    bm, bn = _blk(m, bm), _blk(n, bn)
    return pl.pallas_call(
        _mm_res_kernel,
        grid=(m // bm, n // bn),
        in_specs=[pl.BlockSpec((bm, k), lambda i, j: (i, 0)),
                  pl.BlockSpec((None, k, bn), lambda i, j: (l, 0, j)),
                  pl.BlockSpec((bm, bn), lambda i, j: (i, j))],
        out_specs=pl.BlockSpec((bm, bn), lambda i, j: (i, j)),
        out_shape=jax.ShapeDtypeStruct((m, n), _F32),
        compiler_params=_params("parallel", "parallel"),
        name="matmul_residual",
    )(x, w, res)


def _merge_kernel(ya_ref, yb_ref, pa_ref, pb_ref, ga_ref, gb_ref, o_ref):
    a = jnp.dot(ya_ref[...], pa_ref[...], preferred_element_type=_F32)
    b = jnp.dot(yb_ref[...], pb_ref[...], preferred_element_type=_F32)
    ga = jax.nn.sigmoid(ga_ref[...].astype(_F32))
    gb = jax.nn.sigmoid(gb_ref[...].astype(_F32))
    o_ref[...] = (ga * a + gb * b).astype(o_ref.dtype)


def _merge(ya, yb, pa, pb, l, z, gate_col0, bm=1024, bn=1024):
    m, k = ya.shape
    n = pa.shape[2]
    bm, bn = _blk(m, bm), _blk(math.gcd(n, gate_col0), bn)
    ga0 = gate_col0 // bn
    gb0 = (gate_col0 + n) // bn
    return pl.pallas_call(
        _merge_kernel,
        grid=(m // bm, n // bn),
        in_specs=[pl.BlockSpec((bm, k), lambda i, j: (i, 0)),
                  pl.BlockSpec((bm, k), lambda i, j: (i, 0)),
                  pl.BlockSpec((None, k, bn), lambda i, j: (l, 0, j)),
                  pl.BlockSpec((None, k, bn), lambda i, j: (l, 0, j)),
                  pl.BlockSpec((bm, bn), lambda i, j: (i, ga0 + j)),
                  pl.BlockSpec((bm, bn), lambda i, j: (i, gb0 + j))],
        out_specs=pl.BlockSpec((bm, bn), lambda i, j: (i, j)),
        out_shape=jax.ShapeDtypeStruct((m, n), _BF16),
        compiler_params=_params("parallel", "parallel"),
        name="merge_matmul",
    )(ya, yb, pa, pb, z, z)


def _mixer_kernel(xin_ref, gb_ref, gc_ref, u_ref, v_ref,
                  xin_p_ref, gc_p_ref, xin_n_ref, gc_n_ref,
                  cw_ref, lng_ref, lnb_ref, sw_ref, sb_ref,
                  ya_ref, yb_ref, gv_ref, *, seq_len, chunk, groups):
    i = pl.program_id(0)
    tm, width = xin_ref.shape
    gd = width // groups
    row0 = i * tm
    has_prev = (row0 % seq_len != 0).astype(_F32)
    has_next = ((row0 + tm) % seq_len != 0).astype(_F32)
    rows = lax.broadcasted_iota(jnp.int32, (tm, gd), 0)

    def lane_fold(x):
        return functools.reduce(
            lambda p, q: p + q, [x[:, k * LANES:(k + 1) * LANES] for k in range(gd // LANES)])

    row_sum = jnp.zeros((tm, LANES), _F32)
    for g in range(groups):
        cols = slice(g * gd, (g + 1) * gd)
        zc = gc_ref[:, cols].astype(_F32) * xin_ref[:, cols].astype(_F32)
        zp = gc_p_ref[:, cols].astype(_F32) * xin_p_ref[:, cols].astype(_F32)
        zn = gc_n_ref[:, cols].astype(_F32) * xin_n_ref[:, cols].astype(_F32)
        halo_p = zp[BF16_SUBLANES - 1:BF16_SUBLANES, :] * has_prev
        halo_n = zn[0:1, :] * has_next
        prev = jnp.where(rows == 0, halo_p, pltpu.roll(zc, 1, 0))
        nxt = jnp.where(rows == tm - 1, halo_n, pltpu.roll(zc, tm - 1, 0))
        cw = cw_ref[:, cols]
        conv = prev * cw[0:1, :] + zc * cw[1:2, :] + nxt * cw[2:3, :]
        ya_ref[:, cols] = (gb_ref[:, cols].astype(_F32) * conv).astype(ya_ref.dtype)
        gv = _gelu(v_ref[:, cols].astype(_F32))
        gv_ref[:, cols] = gv
        row_sum = row_sum + lane_fold(gv)

    mu = jnp.sum(row_sum, axis=-1, keepdims=True) * (1.0 / width)
    sq_sum = jnp.zeros((tm, LANES), _F32)
    for g in range(groups):
        vc = gv_ref[:, g * gd:(g + 1) * gd] - mu
        sq_sum = sq_sum + lane_fold(vc * vc)
    rstd = lax.rsqrt(jnp.sum(sq_sum, axis=-1, keepdims=True) * (1.0 / width) + EPS)
    for g in range(groups):
        cols = slice(g * gd, (g + 1) * gd)
        vn = ((gv_ref[:, cols] - mu) * rstd * lng_ref[:, cols] + lnb_ref[:, cols]).astype(_BF16)
        for c in range(tm // chunk):
            r = slice(c * chunk, (c + 1) * chunk)
            mixed = jnp.dot(sw_ref[g], vn[r, :], preferred_element_type=_F32) + sb_ref[:, cols]
            u = _gelu(u_ref[r, cols].astype(_F32))
            yb_ref[r, cols] = (u * mixed).astype(yb_ref.dtype)


def _mixer(z, conv_w, ln_g, ln_b, sg_w, sg_b, seq_len):
    t = z.shape[0]
    width = conv_w.shape[1]
    groups, chunk, _ = sg_w.shape
    assert ln_g.shape[0] == width, "conv and spatial-gating widths must match"
    tm = _blk(seq_len, 2 * chunk)
    assert tm % chunk == 0 and tm % BF16_SUBLANES == 0
    hb = tm // BF16_SUBLANES
    last_hb = t // BF16_SUBLANES - 1
    gd = width // groups
    sb_full = jnp.repeat(sg_b.T, gd, axis=1)

    def col(c):
        return pl.BlockSpec((tm, width), lambda i: (i, c))

    def halo_prev(c):
        return pl.BlockSpec((BF16_SUBLANES, width),
                            lambda i: (jnp.maximum(i * hb - 1, 0), c))

    def halo_next(c):
        return pl.BlockSpec((BF16_SUBLANES, width),
                            lambda i: (jnp.minimum((i + 1) * hb, last_hb), c))

    def full(shape):
        return pl.BlockSpec(shape, lambda i: (0,) * len(shape))

    kern = functools.partial(_mixer_kernel, seq_len=seq_len, chunk=chunk, groups=groups)
    return pl.pallas_call(
        kern,
        grid=(t // tm,),
        in_specs=[col(0), col(1), col(2), col(3), col(4),
                  halo_prev(0), halo_prev(2), halo_next(0), halo_next(2),
                  full((3, width)), full((1, width)), full((1, width)),
                  full((groups, chunk, chunk)), full((chunk, width))],
        out_specs=[pl.BlockSpec((tm, width), lambda i: (i, 0)),
                   pl.BlockSpec((tm, width), lambda i: (i, 0))],
        out_shape=[jax.ShapeDtypeStruct((t, width), _BF16),
                   jax.ShapeDtypeStruct((t, width), _BF16)],
        scratch_shapes=[pltpu.VMEM((tm, width), _F32)],
        compiler_params=_params("parallel"),
        name="mixer",
    )(z, z, z, z, z, z, z, z, z,
      conv_w, ln_g.reshape(1, width), ln_b.reshape(1, width),
      sg_w.astype(_BF16), sb_full)


def _top_rows(sc, k):
    n = sc.shape[0]
    rows = lax.broadcasted_iota(jnp.int32, sc.shape, 0)
    vals, idxs = [], []
    for _ in range(k):
        m = jnp.max(sc, axis=0, keepdims=True)
        idx = jnp.min(jnp.where(sc == m, rows, n), axis=0, keepdims=True)
        sc = jnp.where(rows == idx, -jnp.inf, sc)
        vals.append(m)
        idxs.append(idx)
    return vals, idxs


def _candidate_groups(k):
    return [(k1, k // (k1 + 1)) for k1 in range(k)]


def _query_scores_kernel(x_ref, g_ref, wq_ref, keys_ref, h_ref, sc_ref):
    @pl.when(pl.program_id(1) == 0)
    def _():
        rows = min(x_ref.shape[0], NORM_CHUNK_ROWS)

        def chunk(c, carry):
            r = pl.ds(pl.multiple_of(c * rows, rows), rows)
            _rmsnorm_kernel(x_ref.at[r, :], g_ref, h_ref.at[r, :])
            return carry

        lax.fori_loop(0, x_ref.shape[0] // rows, chunk, 0)

    hk = keys_ref.shape[2]
    q = jnp.dot(h_ref[...], wq_ref[...], preferred_element_type=_F32)
    q_hi, q_lo = _split_bf16(q)
    nt = (((1,), (1,)), ((), ()))
    for hs in range(keys_ref.shape[0]):
        k_hi, k_lo = _split_bf16(keys_ref[hs])
        cols = slice(hs * hk, (hs + 1) * hk)
        sc_ref[hs] = (lax.dot_general(k_hi, q_hi[:, cols], nt, preferred_element_type=_F32)
                      + lax.dot_general(k_hi, q_lo[:, cols], nt, preferred_element_type=_F32)
                      + lax.dot_general(k_lo, q_hi[:, cols], nt, preferred_element_type=_F32))


def _query_scores(x, g, wq, l, keys, bm=512, bn=1024):
    t, d = x.shape
    heads, _, n_keys, hk = keys.shape
    n = heads * 2 * hk
    bm, bn = _blk(t, bm), _blk(n, bn)
    assert bn % hk == 0
    per = bn // hk
    return pl.pallas_call(
        _query_scores_kernel,
        grid=(t // bm, n // bn),
        in_specs=[pl.BlockSpec((bm, d), lambda i, j: (i, 0)),
                  pl.BlockSpec((1, d), lambda i, j: (0, 0)),
                  pl.BlockSpec((None, d, bn), lambda i, j: (l, 0, j)),
                  pl.BlockSpec((per, n_keys, hk), lambda i, j: (j, 0, 0))],
        out_specs=[pl.BlockSpec((bm, d), lambda i, j: (i, 0)),
                   pl.BlockSpec((per, n_keys, bm), lambda i, j: (j, 0, i))],
        out_shape=[jax.ShapeDtypeStruct((t, d), _BF16),
                   jax.ShapeDtypeStruct((heads * 2, n_keys, t), _F32)],
        compiler_params=_params("parallel", "arbitrary"),
        name="query_scores",
    )(x, g.reshape(1, d), wq, keys.reshape(heads * 2, n_keys, hk))


def _retrieve_kernel(sc_ref, i1_ref, i2_ref, g_ref):
    tm = sc_ref.shape[2]
    cw = min(tm, RETRIEVE_CHUNK_LANES)
    for c in range(tm // cw):
        cols = slice(c * cw, (c + 1) * cw)
        i1, i2, g = _retrieve_chunk(sc_ref[0, :, cols], sc_ref[1, :, cols])
        i1_ref[:, cols] = i1
        i2_ref[:, cols] = i2
        g_ref[:, cols] = g


def _retrieve_chunk(sc1, sc2):
    k = PEER_TOPK
    (s1, i1), (s2, i2) = _top_rows(sc1, k), _top_rows(sc2, k)
    cand, c1, c2 = [], [], []
    for k1, n2 in _candidate_groups(k):
        for k2 in range(n2):
            cand.append(s1[k1] + s2[k2])
            c1.append(i1[k1])
            c2.append(i2[k2])
    n_c = len(cand)
    pad = (-n_c) % 8
    lanes = cand[0].shape[1]
    cand = jnp.concatenate(cand + [jnp.full((pad, lanes), -jnp.inf, _F32)], axis=0)
    c1 = jnp.concatenate(c1 + [jnp.zeros((pad, lanes), jnp.int32)], axis=0)
    c2 = jnp.concatenate(c2 + [jnp.zeros((pad, lanes), jnp.int32)], axis=0)
    rows = lax.broadcasted_iota(jnp.int32, cand.shape, 0)
    top, e1, e2 = [], [], []
    for _ in range(k):
        m = jnp.max(cand, axis=0, keepdims=True)
        idx = jnp.min(jnp.where(cand == m, rows, n_c + pad), axis=0, keepdims=True)
        sel = rows == idx
        e1.append(jnp.sum(jnp.where(sel, c1, 0), axis=0, keepdims=True))
        e2.append(jnp.sum(jnp.where(sel, c2, 0), axis=0, keepdims=True))
        cand = jnp.where(sel, -jnp.inf, cand)
        top.append(m)
    top = jnp.concatenate(top, axis=0)
    ex = jnp.exp(top - top[0:1, :])
    gates = ex / jnp.sum(ex, axis=0, keepdims=True)
    return (jnp.concatenate(e1, axis=0).astype(_F32),
            jnp.concatenate(e2, axis=0).astype(_F32), gates)


def _gate_matrix_kernel(i1_ref, i2_ref, g_ref, w_ref, s_ref, *, n_keys):
    tm, slots = i1_ref.shape
    i1 = i1_ref[...]
    i2 = i2_ref[...]
    gt = g_ref[...]
    key_ids = lax.broadcasted_iota(jnp.int32, (n_keys, slots), 0).astype(_F32).astype(_BF16)
    one = jnp.ones((n_keys, slots), _BF16)
    zero = jnp.zeros((n_keys, slots), _BF16)
    grp = BF16_SUBLANES
    n_grp = tm // grp

    ext = n_keys + 2 * BF16_SUBLANES
    ext_ids = (lax.broadcasted_iota(jnp.int32, (ext, slots), 0)
               - W_PITCH_SHIFT).astype(_F32).astype(_BF16)
    ext_one = jnp.ones((ext, slots), _BF16)
    ext_zero = jnp.zeros((ext, slots), _BF16)
    dims = (((1,), (1,)), ((), ()))

    def per_token_products(g):
        for tl in range(grp):
            t = g * grp + tl
            r2 = jnp.broadcast_to(i2[t:t + 1, :].astype(_BF16), (n_keys, slots))
            rg = jnp.broadcast_to(gt[t:t + 1, :].astype(_BF16), (n_keys, slots))
            b_val = jnp.where(key_ids == r2, rg, zero)
            row0 = tl * W_PITCH
            if tl % 2 == 0:
                r1 = jnp.broadcast_to(i1[t:t + 1, :].astype(_BF16), (n_keys, slots))
                a_hot = jnp.where(key_ids == r1, one, zero)
                s_ref[g % 2, row0:row0 + n_keys, :] = lax.dot_general(
                    a_hot, b_val, dims, preferred_element_type=_F32)
            else:
                r1 = jnp.broadcast_to(i1[t:t + 1, :].astype(_BF16), (ext, slots))
                a_hot = jnp.where(ext_ids == r1, ext_one, ext_zero)
                prod = lax.dot_general(a_hot, b_val, dims, preferred_element_type=_F32)
                lo = row0 - W_PITCH_SHIFT
                s_ref[g % 2, lo:lo + n_keys + 8, :] = prod[:n_keys + 8, :]

    def gather_rows(g):
        for a in range(n_keys):
            w_ref[g * grp:(g + 1) * grp, a * n_keys:(a + 1) * n_keys] = (
                s_ref[g % 2, pl.ds(a, grp, stride=W_PITCH), :].astype(w_ref.dtype))

    for g in range(n_grp + 1):
        if g < n_grp:
            per_token_products(g)
        if g > 0:
            gather_rows(g - 1)


def _gate_matrix(i1, i2, gates, n_keys):
    t, slots = i1.shape
    assert n_keys == LANES
    tm = _blk(t, LANES)
    spec = pl.BlockSpec((tm, slots), lambda i: (i, 0))
    return pl.pallas_call(
        functools.partial(_gate_matrix_kernel, n_keys=n_keys),
        grid=(t // tm,),
        in_specs=[spec, spec, spec],
        out_specs=pl.BlockSpec((tm, n_keys * n_keys), lambda i: (i, 0)),
        out_shape=jax.ShapeDtypeStruct((t, n_keys * n_keys), _BF16),
        scratch_shapes=[pltpu.VMEM((2, BF16_SUBLANES * W_PITCH, LANES), _F32)],
        compiler_params=_params("parallel"),
        name="peer_gate_matrix",
    )(i1, i2, gates)


def _experts_kernel(h_ref, u_ref, v_ref, w_ref, x_hbm, o_ref, x_sem, gain_ref=None):
    i, n = pl.program_id(0), pl.program_id(1)
    bm = o_ref.shape[0]

    def residual_copy():
        return pltpu.make_async_copy(
            x_hbm.at[pl.ds(pl.multiple_of(i * bm, bm), bm), :], o_ref, x_sem)

    @pl.when(n == 0)
    def _():
        residual_copy().start()

    bn = u_ref.shape[0]
    sub = min(bn, MXU_COLUMNS)

    parts = bn // sub
    pres = [lax.dot_general(h_ref[...], u_ref[c * sub:(c + 1) * sub, :],
                            (((1,), (1,)), ((), ())), preferred_element_type=_F32)
            for c in range(parts)]

    def gated(c):
        return (w_ref[:, c * sub:(c + 1) * sub].astype(_F32) * _gelu(pres[c])).astype(_BF16)

    act = gated(0)

    @pl.when(n == 0)
    def _():
        residual_copy().wait()

    for c in range(parts):
        o_ref[...] += jnp.dot(act, v_ref[c * sub:(c + 1) * sub, :],
                              preferred_element_type=_F32)
        if c + 1 < parts:
            act = gated(c + 1)

    if gain_ref is not None:
        @pl.when(n == pl.num_programs(1) - 1)
        def _():
            rows = min(bm, NORM_CHUNK_ROWS)

            def chunk(c, carry):
                r = pl.ds(pl.multiple_of(c * rows, rows), rows)
                _rmsnorm_kernel(o_ref.at[r, :], gain_ref, o_ref.at[r, :])
                return carry

            lax.fori_loop(0, bm // rows, chunk, 0)


def _experts_final_kernel(h_ref, u_ref, v_ref, w_ref, x_hbm, gain_ref, o_ref, x_sem):
    _experts_kernel(h_ref, u_ref, v_ref, w_ref, x_hbm, o_ref, x_sem, gain_ref)


def _experts_retrieve_kernel(h_ref, u_ref, v_ref, w_ref, x_hbm, sc_ref,
                             o_ref, i1_ref, i2_ref, g_ref, x_sem, gain_ref=None):
    _retrieve_kernel(sc_ref, i1_ref, i2_ref, g_ref)
    _experts_kernel(h_ref, u_ref, v_ref, w_ref, x_hbm, o_ref, x_sem, gain_ref)


def _experts_retrieve_final_kernel(h_ref, u_ref, v_ref, w_ref, x_hbm, sc_ref, gain_ref,
                                   o_ref, i1_ref, i2_ref, g_ref, x_sem):
    _experts_retrieve_kernel(h_ref, u_ref, v_ref, w_ref, x_hbm, sc_ref,
                             o_ref, i1_ref, i2_ref, g_ref, x_sem, gain_ref)


def _experts_retrieve_cast_kernel(h_ref, u_ref, v_ref, w_ref, x_hbm, sc_ref, src_ref,
                                  o_ref, i1_ref, i2_ref, g_ref, dst_ref, x_sem):
    dst_ref[...] = src_ref[...].astype(dst_ref.dtype)
    _experts_retrieve_kernel(h_ref, u_ref, v_ref, w_ref, x_hbm, sc_ref,
                             o_ref, i1_ref, i2_ref, g_ref, x_sem)


def _experts(x, h, u, v, w, guest=None, cast=None, final_gain=None, bm=1024, bn=512):
    t, d = h.shape
    e = u.shape[0]
    bm, bn = _blk(t, bm), _blk(e, bn)
    steps_n = e // bn
    steps = (t // bm) * steps_n
    in_specs = [pl.BlockSpec((bm, d), lambda i, n: (i, 0)),
                pl.BlockSpec((bn, d), lambda i, n: (n, 0)),
                pl.BlockSpec((bn, d), lambda i, n: (n, 0)),
                pl.BlockSpec((bm, bn), lambda i, n: (i, n)),
                pl.BlockSpec(memory_space=pl.ANY)]
    out_spec = pl.BlockSpec((bm, d), lambda i, n: (i, 0), pipeline_mode=pl.Buffered(1))
    out_shape = jax.ShapeDtypeStruct((t, d), _F32)
    scratch = [pltpu.SemaphoreType.DMA(())]
    gain_spec = pl.BlockSpec((1, d), lambda i, n: (0, 0))
    if guest is None:
        assert cast is None and final_gain is not None
        out = pl.pallas_call(
            _experts_final_kernel,
            grid=(t // bm, steps_n),
            in_specs=in_specs + [gain_spec],
            out_specs=out_spec,
            out_shape=out_shape,
            scratch_shapes=scratch,
            compiler_params=_params("parallel", "arbitrary"),
            name="peer_experts_final",
        )(h, u, v, w, x, final_gain.reshape(1, d))
        return out, None, None

    hs, n_keys, tg = guest.shape
    heads = hs // 2
    k = PEER_TOPK
    assert (tg * heads) % steps == 0
    tmg = tg * heads // steps
    assert tmg % LANES == 0 and tg % tmg == 0

    def tile(i, n):
        return (i * steps_n + n) // heads

    def head(i, n):
        return (i * steps_n + n) % heads

    r_shape = jax.ShapeDtypeStruct((heads * k, tg), _F32)
    r_spec = pl.BlockSpec((k, tmg), lambda i, n: (head(i, n), tile(i, n)))
    in_specs = in_specs + [
        pl.BlockSpec((2, n_keys, tmg), lambda i, n: (head(i, n), 0, tile(i, n)))]
    out_specs = [out_spec, r_spec, r_spec, r_spec]
    out_shapes = [out_shape, r_shape, r_shape, r_shape]
    if cast is None:
        assert final_gain is not None
        out, i1, i2, g = pl.pallas_call(
            _experts_retrieve_final_kernel,
            grid=(t // bm, steps_n),
            in_specs=in_specs + [gain_spec],
            out_specs=out_specs,
            out_shape=out_shapes,
            scratch_shapes=scratch,
            compiler_params=_params("parallel", "arbitrary"),
            name="peer_experts_retrieve_final",
        )(h, u, v, w, x, guest, final_gain.reshape(1, d))
        return out, (i1, i2, g), None

    assert final_gain is None

    table, layer = cast
    rows, cols = table.shape[1:]
    assert rows % steps == 0 and (rows // steps) % BF16_SUBLANES == 0
    slab = rows // steps
    out, i1, i2, g, copy = pl.pallas_call(
        _experts_retrieve_cast_kernel,
        grid=(t // bm, steps_n),
        in_specs=in_specs + [
            pl.BlockSpec((None, slab, cols), lambda i, n: (layer, i * steps_n + n, 0))],
        out_specs=out_specs + [pl.BlockSpec((slab, cols), lambda i, n: (i * steps_n + n, 0))],
        out_shape=out_shapes + [jax.ShapeDtypeStruct((rows, cols), _BF16)],
        scratch_shapes=scratch,
        compiler_params=_params("parallel", "arbitrary"),
        name="peer_experts_retrieve_cast",
    )(h, u, v, w, x, guest, table)
    return out, (i1, i2, g), copy


def _mixer_block(x, h, p, l, seq_len, w_in, scores=None, cast=None, casts=()):
    width = p["conv_w"].shape[-1]
    if scores is not None:
        z, *guest = _mm_hosting(h, *w_in, _BF16, scores, casts)
    elif cast is not None:
        z, guest = _mm(h, *w_in, _BF16, cast=cast)
    else:
        z, guest = _mm(h, *w_in, _BF16), None
    ya, yb = _mixer(z, p["conv_w"][l], p["sg_ln_g"][l], p["sg_ln_b"][l],
                    p["sg_w"][l], p["sg_b"][l], seq_len)
    merged = _merge(ya, yb, p["proj_a"], p["proj_b"], l, z, 5 * width)
    x = _mm_res(merged, p["w_o"], l, x)
    h, sc = _query_scores(x, p["norm2_g"][l], p["peer_wq"], l, p["peer_keys"][l])
    return x, h, sc, guest


def _slots_last(picks):
    return tuple(a.T for a in picks)


def _trunk_pair(xa3, xb3, p):
    depth = p["w_in"].shape[0]
    n_keys = p["peer_keys"].shape[-2]
    d = xa3.shape[-1]
    seq_a, seq_b = xa3.shape[1], xb3.shape[1]
    xa, xb = xa3.reshape(-1, d), xb3.reshape(-1, d)
    ha = _rmsnorm(xa, p["norm1_g"][0], _BF16)
    hb = _rmsnorm(xb, p["norm1_g"][0], _BF16)
    w_in = (p["w_in"][0:1].astype(_BF16), 0)

    def next_w_in(l):
        return (p["w_in"], l + 1) if l + 1 < depth else None

    xa, ha, sa, w_next = _mixer_block(xa, ha, p, 0, seq_a, w_in, cast=next_w_in(0))
    xb, hb, sb, (ra, (u, v)) = _mixer_block(xb, hb, p, 0, seq_b, w_in, scores=sa,
                                            casts=((p["peer_u"], 0), (p["peer_v"], 0)))
    wa = _gate_matrix(*_slots_last(ra), n_keys)
    for l in range(depth):
        last = l + 1 == depth
        if last:
            ha, rb, _ = _experts(xa, ha, u, v, wa, guest=sb, final_gain=p["final_g"])
            wb = _gate_matrix(*_slots_last(rb), n_keys)
            hb, _, _ = _experts(xb, hb, u, v, wb, final_gain=p["final_g"])
        else:
            gain = p["norm1_g"][l + 1]
            xa, rb, u_next = _experts(xa, ha, u, v, wa, guest=sb, cast=(p["peer_u"], l + 1))
            ha = _rmsnorm(xa, gain, _BF16)
            wb = _gate_matrix(*_slots_last(rb), n_keys)
            w_in = (w_next, 0)
            xa, ha, sa, w_next = _mixer_block(xa, ha, p, l + 1, seq_a, w_in,
                                              cast=next_w_in(l + 1))
            xb, ra, v_next = _experts(xb, hb, u, v, wb, guest=sa, cast=(p["peer_v"], l + 1))
            hb = _rmsnorm(xb, gain, _BF16)
            xb, hb, sb, _ = _mixer_block(xb, hb, p, l + 1, seq_b, w_in)
            wa = _gate_matrix(*_slots_last(ra), n_keys)
            u, v = u_next, v_next
    return ha.reshape(xa3.shape), hb.reshape(xb3.shape)


def kernel(x_prompt, x_sample, norm1_g, w_in, conv_w, sg_ln_g, sg_ln_b, sg_w, sg_b, proj_a, proj_b, w_o, norm2_g, peer_wq, peer_keys, peer_u, peer_v, final_g):
    p = dict(
        norm1_g=norm1_g, conv_w=conv_w, sg_ln_g=sg_ln_g, sg_ln_b=sg_ln_b, sg_w=sg_w, sg_b=sg_b,
        norm2_g=norm2_g, peer_keys=peer_keys, final_g=final_g, peer_u=peer_u, peer_v=peer_v,
        w_in=w_in,
        proj_a=proj_a.astype(_BF16), proj_b=proj_b.astype(_BF16),
        w_o=w_o.astype(_BF16), peer_wq=peer_wq.astype(_BF16),
    )
    return _trunk_pair(x_prompt, x_sample, p)
```
